```python
import jax, jax.numpy as jnp
from jax import lax
import numpy as np

D_MODEL = 1024
BATCH = 32
SEQ = 2048
DEPTH = 2

MEM_LEN = 256
HEAD_DIM = 64
N_SB_HEADS = 12
N_MEM_HEADS = 4
DIL_GROUPS = ((128, 1), (512, 4), (2048, 16))
HEADS_PER_GROUP = 4
N_DIL_HEADS = HEADS_PER_GROUP * len(DIL_GROUPS)
SB_WIDTH = N_SB_HEADS * HEAD_DIM
MEM_WIDTH = N_MEM_HEADS * HEAD_DIM
DIL_WIDTH = N_DIL_HEADS * HEAD_DIM
D_FF = 2816
CONV_WIDTH = 3
Q_BLOCK = 128
N_A_LAYERS = DEPTH // 2
N_B_LAYERS = DEPTH - N_A_LAYERS
EPS = 1e-6
ALIBI_MAX_BIAS = 8.0

kernel_name = "yoco_stickbreaking_dilated_hybrid"


def rmsnorm(x, g):
    xf = x.astype(jnp.float32)
    y = xf * lax.rsqrt(jnp.mean(xf * xf, axis=-1, keepdims=True) + EPS)
    return (y * g.astype(jnp.float32)).astype(x.dtype)


def alibi_slopes(n):
    return 2.0 ** (-ALIBI_MAX_BIAS * jnp.arange(1, n + 1, dtype=jnp.float32) / n)


def _heads(t, n_heads):
    return t.reshape(t.shape[0], t.shape[1], n_heads, HEAD_DIM)


def stick_breaking_attention(q, k, v):
    s_len = q.shape[1]
    scale = HEAD_DIM ** -0.5
    outs = []
    for blk in range(s_len // Q_BLOCK):
        q0 = blk * Q_BLOCK
        k_end = q0 + Q_BLOCK
        z = jnp.einsum('bqhd,bkhd->bhqk', q[:, q0:k_end], k[:, :k_end]).astype(jnp.float32) * scale
        t_pos = q0 + jnp.arange(Q_BLOCK)[:, None]
        s_pos = jnp.arange(k_end)[None, :]
        causal = s_pos < t_pos
        log_stay = jnp.where(causal, -jax.nn.softplus(z), 0.0)
        later = lax.cumsum(log_stay, axis=3, reverse=True) - log_stay
        w = jnp.where(causal, jnp.exp(jax.nn.log_sigmoid(z) + later), 0.0)
        outs.append(jnp.einsum('bhqk,bkhd->bqhd', w.astype(v.dtype), v[:, :k_end]))
    return jnp.concatenate(outs, axis=1)


def dilated_window_attention(q, k, v, slopes, window, dilation):
    b, s_len, n_h, dh = q.shape
    w_sub = window // dilation
    blk = w_sub
    L = s_len // dilation
    nb = -(-L // blk)
    Lp = nb * blk

    def by_residue(t):
        return t.reshape(b, L, dilation, n_h, dh).transpose(0, 2, 1, 3, 4)

    qs, ks, vs = by_residue(q), by_residue(k), by_residue(v)
    qb = jnp.pad(qs, ((0, 0), (0, 0), (0, Lp - L), (0, 0), (0, 0))).reshape(b, dilation, nb, blk, n_h, dh)

    def key_blocks(t):
        tp = jnp.pad(t, ((0, 0), (0, 0), (blk, Lp - L), (0, 0), (0, 0)))
        prev = tp[:, :, :Lp].reshape(b, dilation, nb, blk, n_h, dh)
        cur = tp[:, :, blk:].reshape(b, dilation, nb, blk, n_h, dh)
        return jnp.concatenate([prev, cur], axis=3)

    kb, vb = key_blocks(ks), key_blocks(vs)
    sc = jnp.einsum('brnqhd,brnkhd->brnhqk', qb, kb).astype(jnp.float32) * dh ** -0.5
    n_idx = jnp.arange(nb)[:, None, None]
    i_idx = jnp.arange(blk)[None, :, None]
    j_idx = jnp.arange(2 * blk)[None, None, :]
    delta = i_idx + blk - j_idx
    valid = (delta >= 0) & (delta <= w_sub) & (n_idx * blk - blk + j_idx >= 0)
    bias = -slopes[None, :, None, None] * (delta * dilation).astype(jnp.float32)[:, None]
    sc = jnp.where(valid[:, None], sc + bias, -jnp.inf)
    m = jnp.max(sc, axis=-1, keepdims=True)
    p = jnp.exp(sc - m)
    denom = jnp.sum(p, axis=-1)
    o = jnp.einsum('brnhqk,brnkhd->brnqhd', p.astype(v.dtype), vb).astype(jnp.float32)
    o = o / jnp.moveaxis(denom, 3, 4)[..., None]
    lse = jnp.moveaxis(m[..., 0] + jnp.log(denom), 3, 4)

    def back(t):
        t = t.reshape((b, dilation, Lp) + t.shape[4:])[:, :, :L]
        t = jnp.swapaxes(t, 1, 2)
        return t.reshape((b, s_len) + t.shape[3:])

    return back(o).astype(q.dtype), back(lse)


def memory_branch(q_mem, mem, norm_mem, w_mem_kv):
    k_m, v_m = jnp.split(rmsnorm(mem, norm_mem) @ w_mem_kv, 2, axis=-1)
    q, k, v = _heads(q_mem, N_MEM_HEADS), _heads(k_m, N_MEM_HEADS), _heads(v_m, N_MEM_HEADS)
    sc = jnp.einsum('bqhd,bkhd->bhqk', q, k).astype(jnp.float32) * HEAD_DIM ** -0.5
    p = jax.nn.softmax(sc, axis=-1)
    o = jnp.einsum('bhqk,bkhd->bqhd', p.astype(v.dtype), v)
    return o.reshape(q_mem.shape)


def conv_ffn(x, w_up, w_conv, w_down):
    s_len = x.shape[1]
    u = x @ w_up
    up = jnp.pad(u, ((0, 0), (CONV_WIDTH - 1, 0), (0, 0)))
    c = w_conv[0] * up[:, 0:s_len]
    for j in range(1, CONV_WIDTH):
        c = c + w_conv[j] * up[:, j:j + s_len]
    a, g = jnp.split(c, 2, axis=-1)
    return (jax.nn.silu(g) * a) @ w_down


def self_decoder_layer(x, mem, norm_attn, w_in, w_out, norm_mem, w_mem_kv, norm_ffn, ffn_up, ffn_conv, ffn_down):
    b, s_len, _ = x.shape
    proj = rmsnorm(x, norm_attn) @ w_in
    q_sb, k_sb, v_sb, q_mem = jnp.split(proj, [SB_WIDTH, 2 * SB_WIDTH, 3 * SB_WIDTH], axis=-1)
    o_sb = stick_breaking_attention(_heads(q_sb, N_SB_HEADS), _heads(k_sb, N_SB_HEADS), _heads(v_sb, N_SB_HEADS))
    o_mem = memory_branch(q_mem, mem, norm_mem, w_mem_kv)
    x = x + jnp.concatenate([o_sb.reshape(b, s_len, SB_WIDTH), o_mem], axis=-1) @ w_out
    return x + conv_ffn(rmsnorm(x, norm_ffn), ffn_up, ffn_conv, ffn_down)


def cross_decoder_layer(x, k_sh, v_sh, mem, norm_attn, w_in, w_out, norm_mem, w_mem_kv, norm_ffn, ffn_up, ffn_conv, ffn_down):
    b, s_len, _ = x.shape
    proj = rmsnorm(x, norm_attn) @ w_in
    q_dil, q_mem = jnp.split(proj, [DIL_WIDTH], axis=-1)
    q_dil = _heads(q_dil, N_DIL_HEADS)
    slopes = alibi_slopes(N_DIL_HEADS)
    outs, lses = [], []
    for g, (window, dilation) in enumerate(DIL_GROUPS):
        hs = slice(g * HEADS_PER_GROUP, (g + 1) * HEADS_PER_GROUP)
        o_g, lse_g = dilated_window_attention(q_dil[:, :, hs], k_sh[:, :, hs], v_sh[:, :, hs], slopes[hs], window, dilation)
        outs.append(o_g)
        lses.append(lse_g)
    alpha = jax.nn.softmax(jnp.stack(lses, axis=0), axis=0)
    o_dil = jnp.concatenate([o * alpha[g][..., None].astype(o.dtype) for g, o in enumerate(outs)], axis=2)
    o_mem = memory_branch(q_mem, mem, norm_mem, w_mem_kv)
    x = x + jnp.concatenate([o_dil.reshape(b, s_len, DIL_WIDTH), o_mem], axis=-1) @ w_out
    return x + conv_ffn(rmsnorm(x, norm_ffn), ffn_up, ffn_conv, ffn_down)


def setup_inputs(seed: int = 0) -> dict:
    key = jax.random.key(seed)
    ks = jax.random.split(key, 24)

    def w(k, shape, fan_in):
        return jax.random.normal(k, shape, jnp.float32) * fan_in ** -0.5

    def gain(k, shape):
        return 1.0 + 0.02 * jax.random.normal(k, shape, jnp.float32)

    na, nb = N_A_LAYERS, N_B_LAYERS
    return {
        'x': jax.random.normal(ks[0], (BATCH, SEQ, D_MODEL), jnp.float32),
        'mem': jax.random.normal(ks[1], (BATCH, MEM_LEN, D_MODEL), jnp.float32),
        'a_norm_attn': gain(ks[2], (na, D_MODEL)),
        'a_w_in': w(ks[3], (na, D_MODEL, 3 * SB_WIDTH + MEM_WIDTH), D_MODEL),
        'a_w_out': w(ks[4], (na, SB_WIDTH + MEM_WIDTH, D_MODEL), SB_WIDTH + MEM_WIDTH),
        'a_norm_mem': gain(ks[5], (na, D_MODEL)),
        'a_w_mem_kv': w(ks[6], (na, D_MODEL, 2 * MEM_WIDTH), D_MODEL),
        'a_norm_ffn': gain(ks[7], (na, D_MODEL)),
        'a_ffn_up': w(ks[8], (na, D_MODEL, 2 * D_FF), D_MODEL),
        'a_ffn_conv': w(ks[9], (na, CONV_WIDTH, 2 * D_FF), CONV_WIDTH),
        'a_ffn_down': w(ks[10], (na, D_FF, D_MODEL), D_FF),
        'kv_norm': gain(ks[11], (D_MODEL,)),
        'w_kv_shared': w(ks[12], (D_MODEL, 2 * DIL_WIDTH), D_MODEL),
        'b_norm_attn': gain(ks[13], (nb, D_MODEL)),
        'b_w_in': w(ks[14], (nb, D_MODEL, DIL_WIDTH + MEM_WIDTH), D_MODEL),
        'b_w_out': w(ks[15], (nb, DIL_WIDTH + MEM_WIDTH, D_MODEL), DIL_WIDTH + MEM_WIDTH),
        'b_norm_mem': gain(ks[16], (nb, D_MODEL)),
        'b_w_mem_kv': w(ks[17], (nb, D_MODEL, 2 * MEM_WIDTH), D_MODEL),
        'b_norm_ffn': gain(ks[18], (nb, D_MODEL)),
        'b_ffn_up': w(ks[19], (nb, D_MODEL, 2 * D_FF), D_MODEL),
        'b_ffn_conv': w(ks[20], (nb, CONV_WIDTH, 2 * D_FF), CONV_WIDTH),
        'b_ffn_down': w(ks[21], (nb, D_FF, D_MODEL), D_FF),
        'final_norm': gain(ks[22], (D_MODEL,)),
    }


def reference(x, mem, a_norm_attn, a_w_in, a_w_out, a_norm_mem, a_w_mem_kv, a_norm_ffn, a_ffn_up, a_ffn_conv, a_ffn_down,
              kv_norm, w_kv_shared, b_norm_attn, b_w_in, b_w_out, b_norm_mem, b_w_mem_kv, b_norm_ffn, b_ffn_up, b_ffn_conv,
              b_ffn_down, final_norm):
    b, s_len, _ = x.shape
    h = x
    k_sh = None
    v_sh = None
    for layer in range(DEPTH):
        if layer < N_A_LAYERS:
            i = layer
            h = self_decoder_layer(h, mem, a_norm_attn[i], a_w_in[i], a_w_out[i], a_norm_mem[i], a_w_mem_kv[i],
                                   a_norm_ffn[i], a_ffn_up[i], a_ffn_conv[i], a_ffn_down[i])
            if layer == N_A_LAYERS - 1:
                k_flat, v_flat = jnp.split(rmsnorm(h, kv_norm) @ w_kv_shared, 2, axis=-1)
                k_sh = _heads(k_flat, N_DIL_HEADS)
                v_sh = _heads(v_flat, N_DIL_HEADS)
        else:
            j = layer - N_A_LAYERS
            h = cross_decoder_layer(h, k_sh, v_sh, mem, b_norm_attn[j], b_w_in[j], b_w_out[j], b_norm_mem[j],
                                    b_w_mem_kv[j], b_norm_ffn[j], b_ffn_up[j], b_ffn_conv[j], b_ffn_down[j])
    return rmsnorm(h, final_norm)
```

```python
import functools

import jax
import jax.numpy as jnp
from jax import lax
from jax.experimental import pallas as pl
from jax.experimental.pallas import tpu as pltpu

D_MODEL = 1024
HEAD_DIM = 64
N_SB_HEADS = 12
N_MEM_HEADS = 4
DIL_GROUPS = ((128, 1), (512, 4), (2048, 16))
HEADS_PER_GROUP = 4
N_DIL_HEADS = HEADS_PER_GROUP * len(DIL_GROUPS)
SB_WIDTH = N_SB_HEADS * HEAD_DIM
MEM_WIDTH = N_MEM_HEADS * HEAD_DIM
DIL_WIDTH = N_DIL_HEADS * HEAD_DIM
GROUP_WIDTH = HEADS_PER_GROUP * HEAD_DIM
D_FF = 2816
CONV_WIDTH = 3
EPS = 1e-6
ALIBI_MAX_BIAS = 8.0
QK_SCALE = HEAD_DIM ** -0.5

LANES = 128
BF16_ROWS = 16
VMEM_LIMIT_BYTES = 56 * 1024 * 1024

BLK = 128
F32 = jnp.float32
BF16 = jnp.bfloat16
NT_DIMS = (((1,), (1,)), ((), ()))


def _params(semantics):
    return pltpu.CompilerParams(dimension_semantics=semantics, vmem_limit_bytes=VMEM_LIMIT_BYTES)


def _resident(shape):
    return pl.BlockSpec(shape, lambda *_: (0,) * len(shape), pipeline_mode=pl.Buffered(1))


def _head0_lanes():
    return lax.broadcasted_iota(jnp.int32, (1, LANES), 1) < HEAD_DIM


def _split_heads(t, head0):
    zero = jnp.zeros_like(t)
    return jnp.concatenate([jnp.where(head0, t, zero), jnp.where(head0, zero, t)], axis=0)


def _rms_proj_kernel(n_out, x_ref, *refs):
    g_refs, w_refs, o_refs = refs[:n_out], refs[n_out:2 * n_out], refs[2 * n_out:]
    x = x_ref[...]
    xhat = x * lax.rsqrt(jnp.mean(x * x, axis=-1, keepdims=True) + EPS)
    for g_ref, w_ref, o_ref in zip(g_refs, w_refs, o_refs):
        xn = (xhat * g_ref[...]).astype(BF16)
        o_ref[...] = jnp.dot(xn, w_ref[...], preferred_element_type=F32).astype(o_ref.dtype)


def _rms_proj(x, gains, weights, tm):
    t, d = x.shape
    n_out = len(gains)
    in_specs = [pl.BlockSpec((tm, d), lambda i: (i, 0))]
    in_specs += [_resident((1, d)) for _ in gains]
    in_specs += [_resident(w.shape) for w in weights]
    out_specs = [pl.BlockSpec((tm, w.shape[1]), lambda i: (i, 0)) for w in weights]
    out_shape = [jax.ShapeDtypeStruct((t, w.shape[1]), BF16) for w in weights]
    return pl.pallas_call(
        functools.partial(_rms_proj_kernel, n_out),
        grid=(t // tm,),
        in_specs=in_specs,
        out_specs=out_specs,
        out_shape=out_shape,
        compiler_params=_params(("parallel",)),
        name="rms_proj",
    )(x, *[g.reshape(1, d) for g in gains], *weights)


SB_ROWS = 256


def _sb_kernel(q_ref, k_ref, v_ref, tri_ref, o_ref, carry_ref, acc_ref):
    s_len = q_ref.shape[0]
    n_kb = s_len // BLK
    n_tiles = s_len // SB_ROWS
    head0 = _head0_lanes()
    tri = tri_ref[...]
    carry_ref[...] = jnp.zeros_like(carry_ref)
    acc_ref[...] = jnp.zeros_like(acc_ref)

    def key_block(jj, _):
        j = n_kb - 1 - jj
        k0 = pl.multiple_of(j * BLK, BLK)
        kst = _split_heads(k_ref[pl.ds(k0, BLK), :], head0)
        vst = _split_heads(v_ref[pl.ds(k0, BLK), :], head0)

        def tile(c, masked):
            r0 = pl.multiple_of(c * SB_ROWS, SB_ROWS)
            rows = pl.ds(r0, SB_ROWS)
            q = q_ref[rows, :] * QK_SCALE
            z = lax.dot_general(q, kst, NT_DIMS, preferred_element_type=F32)
            sp = jnp.maximum(z, 0.0) + jnp.log(1.0 + jnp.exp(-jnp.abs(z)))
            if masked:
                t_pos = r0 + lax.broadcasted_iota(jnp.int32, z.shape, 0)
                s_pos = k0 + (lax.broadcasted_iota(jnp.int32, z.shape, 1) & (BLK - 1))
                causal = s_pos < t_pos
                sp = jnp.where(causal, sp, 0.0)
            hi = sp.astype(BF16)
            lo = (sp - hi.astype(F32)).astype(BF16)
            sums = (jnp.dot(hi, tri, preferred_element_type=F32)
                    + jnp.dot(lo, tri, preferred_element_type=F32))
            w = jnp.exp(z - (sums[:, :2 * BLK] + carry_ref[rows, :]))
            if masked:
                w = jnp.where(causal, w, 0.0)
            carry_ref[rows, :] += sums[:, 2 * BLK:]
            acc_ref[rows, :] += jnp.dot(w.astype(BF16), vst, preferred_element_type=F32)

        c_first = (j * BLK) // SB_ROWS
        tile(c_first, True)
        lax.fori_loop(c_first + 1, n_tiles, lambda c, x: (tile(c, False), x)[1], 0)
        return 0

    lax.fori_loop(0, n_kb, key_block, 0)
    o_ref[...] = acc_ref[...].astype(o_ref.dtype)


def _sb_tri_weights():
    r = jnp.arange(2 * BLK)[:, None]
    c = jnp.arange(2 * BLK)[None, :]
    same_head = (r // BLK) == (c // BLK)
    suffix = same_head & (r >= c)
    return jnp.concatenate([suffix, same_head], axis=1).astype(BF16)


def _sb_attn(proj, batch, s_len):
    n_pairs = SB_WIDTH // LANES
    blk = (s_len, LANES)
    return pl.pallas_call(
        _sb_kernel,
        grid=(batch, n_pairs),
        in_specs=[
            pl.BlockSpec(blk, lambda b, p: (b, p)),
            pl.BlockSpec(blk, lambda b, p: (b, n_pairs + p)),
            pl.BlockSpec(blk, lambda b, p: (b, 2 * n_pairs + p)),
            _resident((2 * BLK, 4 * BLK)),
        ],
        out_specs=pl.BlockSpec(blk, lambda b, p: (b, p)),
        out_shape=jax.ShapeDtypeStruct((batch * s_len, SB_WIDTH), BF16),
        scratch_shapes=[pltpu.VMEM((s_len, 2 * BLK), F32), pltpu.VMEM((s_len, LANES), F32)],
        compiler_params=_params(("parallel", "parallel")),
        name="sb_attn",
    )(proj, proj, proj, _sb_tri_weights())


MEM_ROWS = 256


def _mem_kernel(q_ref, kv_ref, o_ref):
    s_len = q_ref.shape[0]
    mem_len = kv_ref.shape[0]
    head0 = _head0_lanes()
    for p in range(MEM_WIDTH // LANES):
        kst = _split_heads(kv_ref[:, p * LANES:(p + 1) * LANES], head0)
        vst = _split_heads(kv_ref[:, MEM_WIDTH + p * LANES:MEM_WIDTH + (p + 1) * LANES], head0)

        def step(c, carry):
            rows = pl.ds(pl.multiple_of(c * MEM_ROWS, MEM_ROWS), MEM_ROWS)
            q = q_ref[rows, p * LANES:(p + 1) * LANES] * QK_SCALE
            z = lax.dot_general(q, kst, NT_DIMS, preferred_element_type=F32)
            ps, dens = [], []
            for h in range(2):
                zh = z[:, h * mem_len:(h + 1) * mem_len]
                e = jnp.exp(zh - jnp.max(zh, axis=-1, keepdims=True))
                ps.append(e)
                dens.append(jnp.sum(e, axis=-1, keepdims=True))
            o = jnp.dot(jnp.concatenate(ps, axis=1).astype(BF16), vst, preferred_element_type=F32)
            o = o / jnp.where(head0, dens[0], dens[1])
            o_ref[rows, p * LANES:(p + 1) * LANES] = o.astype(o_ref.dtype)
            return carry

        lax.fori_loop(0, s_len // MEM_ROWS, step, 0)


def _mem_attn(proj, q_col_block, kv_mem, batch, s_len):
    mem_len = kv_mem.shape[0] // batch
    return pl.pallas_call(
        _mem_kernel,
        grid=(batch,),
        in_specs=[
            pl.BlockSpec((s_len, MEM_WIDTH), lambda b: (b, q_col_block)),
            pl.BlockSpec((mem_len, 2 * MEM_WIDTH), lambda b: (b, 0)),
        ],
        out_specs=pl.BlockSpec((s_len, MEM_WIDTH), lambda b: (b, 0)),
        out_shape=jax.ShapeDtypeStruct((batch * s_len, MEM_WIDTH), BF16),
        compiler_params=_params(("parallel",)),
        name="mem_attn",
    )(proj, kv_mem)


def _dil_kernel(q_ref, k_ref, v_ref, bias_ref, o_ref, lse_ref):
    n_rows = q_ref.shape[0]
    head0 = _head0_lanes()

    for p in range(GROUP_WIDTH // LANES):
        cols = slice(p * LANES, (p + 1) * LANES)

        def block(n, first):
            r0 = pl.multiple_of(n * BLK, BLK)
            rp = r0 if first else pl.multiple_of(r0 - BLK, BLK)
            cur, prev = pl.ds(r0, BLK), pl.ds(rp, BLK)
            q = q_ref[cur, cols] * QK_SCALE
            kst = jnp.concatenate([_split_heads(k_ref[prev, cols], head0),
                                   _split_heads(k_ref[cur, cols], head0)], axis=0)
            vst = jnp.concatenate([_split_heads(v_ref[prev, cols], head0),
                                   _split_heads(v_ref[cur, cols], head0)], axis=0)
            z = lax.dot_general(q, kst, NT_DIMS, preferred_element_type=F32)
            z = z + bias_ref[p, 0 if first else 1]
            ps, dens, lses = [], [], []
            for h in range(2):
                zp, zc = z[:, h * BLK:(h + 1) * BLK], z[:, (2 + h) * BLK:(3 + h) * BLK]
                m = jnp.max(jnp.maximum(zp, zc), axis=-1, keepdims=True)
                ep, ec = jnp.exp(zp - m), jnp.exp(zc - m)
                den = jnp.sum(ep + ec, axis=-1, keepdims=True)
                ps.append((ep, ec))
                dens.append(den)
                lses.append(m + jnp.log(den))
            pmat = jnp.concatenate([ps[0][0], ps[1][0], ps[0][1], ps[1][1]], axis=1).astype(BF16)
            o = jnp.dot(pmat, vst, preferred_element_type=F32)
            o_ref[cur, cols] = o / jnp.where(head0, dens[0], dens[1])
            lse_ref[cur, cols] = jnp.where(head0, lses[0], lses[1])

        block(0, True)
        if n_rows > BLK:
            lax.fori_loop(1, n_rows // BLK, lambda n, x: (block(n, False), x)[1], 0)


def _dil_bias(slopes, dilation):
    i = jnp.arange(BLK)[:, None]
    j = jnp.arange(2 * BLK)[None, :]
    delta = i + BLK - j
    valid = (delta >= 0) & (delta <= BLK)
    valid_first = valid & (j >= BLK)
    dist = (delta * dilation).astype(F32)
    out = []
    for p in range(2):
        per_first = []
        for ok in (valid_first, valid):
            halves = [jnp.where(ok, -slopes[2 * p + h] * dist, -jnp.inf) for h in range(2)]
            per_first.append(jnp.concatenate(
                [halves[0][:, :BLK], halves[1][:, :BLK], halves[0][:, BLK:], halves[1][:, BLK:]], axis=1))
        out.append(jnp.stack(per_first))
    return jnp.stack(out)


def _dil_attn(proj, kv, group, slopes, batch, s_len):
    _, d = DIL_GROUPS[group]
    n_rows = s_len // d
    q_blocks = proj.shape[1] // GROUP_WIDTH
    kv_blocks = kv.shape[1] // GROUP_WIDTH
    v_off = DIL_WIDTH // GROUP_WIDTH
    t = batch * s_len
    blk = (n_rows, GROUP_WIDTH)
    o, lse = pl.pallas_call(
        _dil_kernel,
        grid=(batch, d),
        in_specs=[
            pl.BlockSpec(blk, lambda b, r: (b, r * q_blocks + group)),
            pl.BlockSpec(blk, lambda b, r: (b, r * kv_blocks + group)),
            pl.BlockSpec(blk, lambda b, r: (b, r * kv_blocks + v_off + group)),
            _resident((2, 2, BLK, 4 * BLK)),
        ],
        out_specs=[pl.BlockSpec(blk, lambda b, r: (b, r))] * 2,
        out_shape=[jax.ShapeDtypeStruct((t // d, d * GROUP_WIDTH), F32)] * 2,
        compiler_params=_params(("parallel", "parallel")),
        name=f"dil_attn_g{group}",
    )(proj.reshape(t // d, d * proj.shape[1]), kv.reshape(t // d, d * kv.shape[1]),
      kv.reshape(t // d, d * kv.shape[1]), _dil_bias(slopes[group * 4:(group + 1) * 4], d))
    return o.reshape(t, GROUP_WIDTH), lse.reshape(t, GROUP_WIDTH)


def _dil_combine_kernel(o0, l0, o1, l1, o2, l2, y0, y1, y2):
    ls = [l0[...], l1[...], l2[...]]
    m = jnp.maximum(jnp.maximum(ls[0], ls[1]), ls[2])
    es = [jnp.exp(l - m) for l in ls]
    tot = es[0] + es[1] + es[2]
    for o_ref, e, y_ref in zip((o0, o1, o2), es, (y0, y1, y2)):
        y_ref[...] = (o_ref[...] * (e / tot)).astype(y_ref.dtype)


def _dil_combine(parts, tm):
    t = parts[0][0].shape[0]
    spec = pl.BlockSpec((tm, GROUP_WIDTH), lambda i: (i, 0))
    return pl.pallas_call(
        _dil_combine_kernel,
        grid=(t // tm,),
        in_specs=[spec] * 6,
        out_specs=[spec] * 3,
        out_shape=[jax.ShapeDtypeStruct((t, GROUP_WIDTH), BF16)] * 3,
        compiler_params=_params(("parallel",)),
        name="dil_combine",
    )(*[a for part in parts for a in part])


FFN_ROWS = 512
FFN_CHUNK = 256
HALO = BF16_ROWS


def _out_ffn_kernel(n_attn, tiles_per_seq, final, h_ref, *refs):
    a_refs, wo_refs = refs[:n_attn], refs[n_attn:2 * n_attn]
    g_ref, wup_ref, wconv_ref, wdown_ref = refs[2 * n_attn:2 * n_attn + 4]
    rest = refs[2 * n_attn + 4:]
    if final:
        gf_ref, out_ref, xe_ref, act_ref = rest
    else:
        out_ref, xe_ref, act_ref = rest
    tm = h_ref.shape[0]
    i = pl.program_id(0)

    h = h_ref[...]
    for a_ref, wo_ref in zip(a_refs, wo_refs):
        h = h + jnp.dot(a_ref[...], wo_ref[...], preferred_element_type=F32)
    xn = h * lax.rsqrt(jnp.mean(h * h, axis=-1, keepdims=True) + EPS) * g_ref[...]

    @pl.when(i % tiles_per_seq == 0)
    def _():
        xe_ref[:HALO, :] = jnp.zeros((HALO, xe_ref.shape[1]), xe_ref.dtype)

    @pl.when(i % tiles_per_seq != 0)
    def _():
        xe_ref[:HALO, :] = xe_ref[tm:, :]

    xe_ref[HALO:, :] = xn.astype(xe_ref.dtype)

    def conv(u, wc):
        return (wc[0:1] * pltpu.roll(u, 2, axis=0)[HALO:] + wc[1:2] * pltpu.roll(u, 1, axis=0)[HALO:]
                + wc[2:3] * u[HALO:])

    xe = xe_ref[...]
    for c in range(D_FF // FFN_CHUNK):
        ca_cols = slice(c * FFN_CHUNK, (c + 1) * FFN_CHUNK)
        cg_cols = slice(D_FF + c * FFN_CHUNK, D_FF + (c + 1) * FFN_CHUNK)
        ca = conv(jnp.dot(xe, wup_ref[:, ca_cols], preferred_element_type=F32), wconv_ref[:, ca_cols])
        cg = conv(jnp.dot(xe, wup_ref[:, cg_cols], preferred_element_type=F32), wconv_ref[:, cg_cols])
        act_ref[:, ca_cols] = (cg * jax.nn.sigmoid(cg) * ca).astype(act_ref.dtype)

    y = h + jnp.dot(act_ref[...], wdown_ref[...], preferred_element_type=F32)
    if final:
        y = y * lax.rsqrt(jnp.mean(y * y, axis=-1, keepdims=True) + EPS) * gf_ref[...]
    out_ref[...] = y


def _out_ffn(h, attn_parts, w_out, g_ffn, w_up, w_conv, w_down, s_len, g_final=None):
    t, d = h.shape
    tm = FFN_ROWS
    n_attn = len(attn_parts)
    final = g_final is not None
    offs = [0]
    for a in attn_parts:
        offs.append(offs[-1] + a.shape[1])
    wo_parts = [w_out[offs[k]:offs[k + 1]].astype(BF16) for k in range(n_attn)]
    row = lambda n: pl.BlockSpec((tm, n), lambda i: (i, 0))
    in_specs = [row(d)] + [row(a.shape[1]) for a in attn_parts] + [_resident(w.shape) for w in wo_parts]
    in_specs += [_resident((1, d)), _resident(w_up.shape), _resident(w_conv.shape), _resident(w_down.shape)]
    args = [h, *attn_parts, *wo_parts, g_ffn.reshape(1, d), w_up.astype(BF16), w_conv, w_down.astype(BF16)]
    if final:
        in_specs.append(_resident((1, d)))
        args.append(g_final.reshape(1, d))
    return pl.pallas_call(
        functools.partial(_out_ffn_kernel, n_attn, s_len // tm, final),
        grid=(t // tm,),
        in_specs=in_specs,
        out_specs=row(d),
        out_shape=jax.ShapeDtypeStruct((t, d), F32),
        scratch_shapes=[pltpu.VMEM((HALO + tm, d), BF16), pltpu.VMEM((tm, D_FF), BF16)],
        compiler_params=_params(("arbitrary",)),
        name="out_ffn",
    )(*args)


PROJ_ROWS = 512


def kernel(x, mem, a_norm_attn, a_w_in, a_w_out, a_norm_mem, a_w_mem_kv, a_norm_ffn, a_ffn_up, a_ffn_conv, a_ffn_down, kv_norm, w_kv_shared, b_norm_attn, b_w_in, b_w_out, b_norm_mem, b_w_mem_kv, b_norm_ffn, b_ffn_up, b_ffn_conv, b_ffn_down, final_norm):
    batch, s_len, d = x.shape
    assert a_w_in.shape[0] == 1 and b_w_in.shape[0] == 1, "one self-decoder and one cross-decoder layer"
    assert d == D_MODEL and s_len % (BLK * DIL_GROUPS[-1][1]) == 0 and s_len % FFN_ROWS == 0
    t = batch * s_len
    h = x.reshape(t, d)
    mem2 = mem.reshape(batch * mem.shape[1], d)

    kvm_a, kvm_b = _rms_proj(mem2, [a_norm_mem[0], b_norm_mem[0]],
                             [a_w_mem_kv[0].astype(BF16), b_w_mem_kv[0].astype(BF16)], PROJ_ROWS)

    (proj_a,) = _rms_proj(h, [a_norm_attn[0]], [a_w_in[0].astype(BF16)], PROJ_ROWS)
    o_sb = _sb_attn(proj_a, batch, s_len)
    o_mem = _mem_attn(proj_a, 3 * SB_WIDTH // MEM_WIDTH, kvm_a, batch, s_len)
    h = _out_ffn(h, [o_sb, o_mem], a_w_out[0], a_norm_ffn[0], a_ffn_up[0], a_ffn_conv[0], a_ffn_down[0], s_len)

    kv, proj_b = _rms_proj(h, [kv_norm, b_norm_attn[0]], [w_kv_shared.astype(BF16), b_w_in[0].astype(BF16)],
                           PROJ_ROWS)
    slopes = 2.0 ** (-ALIBI_MAX_BIAS * jnp.arange(1, N_DIL_HEADS + 1, dtype=F32) / N_DIL_HEADS)
    parts = [_dil_attn(proj_b, kv, g, slopes, batch, s_len) for g in range(len(DIL_GROUPS))]
    o_dil = _dil_combine(parts, 1024)
    o_mem = _mem_attn(proj_b, DIL_WIDTH // MEM_WIDTH, kvm_b, batch, s_len)
    h = _out_ffn(h, [*o_dil, o_mem], b_w_out[0], b_norm_ffn[0], b_ffn_up[0], b_ffn_conv[0], b_ffn_down[0], s_len,
                 g_final=final_norm)
    return h.reshape(batch, s_len, d)
```

```python
import functools

import jax
import jax.numpy as jnp
from jax import lax
from jax.experimental import pallas as pl
from jax.experimental.pallas import tpu as pltpu

D_MODEL = 1024
HEAD_DIM = 64
N_SB_HEADS = 12
N_MEM_HEADS = 4
DIL_GROUPS = ((128, 1), (512, 4), (2048, 16))
HEADS_PER_GROUP = 4
N_DIL_HEADS = HEADS_PER_GROUP * len(DIL_GROUPS)
SB_WIDTH = N_SB_HEADS * HEAD_DIM
MEM_WIDTH = N_MEM_HEADS * HEAD_DIM
DIL_WIDTH = N_DIL_HEADS * HEAD_DIM
GROUP_WIDTH = HEADS_PER_GROUP * HEAD_DIM
D_FF = 2816
CONV_WIDTH = 3
EPS = 1e-6
ALIBI_MAX_BIAS = 8.0
QK_SCALE = HEAD_DIM ** -0.5
LOG2E = 1.4426950408889634

LANES = 128
BF16_ROWS = 16
VMEM_LIMIT_BYTES = 56 * 1024 * 1024

BLK = 128
F32 = jnp.float32
BF16 = jnp.bfloat16
NT_DIMS = (((1,), (1,)), ((), ()))


def _params(semantics):
    return pltpu.CompilerParams(dimension_semantics=semantics, vmem_limit_bytes=VMEM_LIMIT_BYTES)


def _resident(shape):
    return pl.BlockSpec(shape, lambda *_: (0,) * len(shape), pipeline_mode=pl.Buffered(1))


def _head0_lanes():
    return lax.broadcasted_iota(jnp.int32, (1, LANES), 1) < HEAD_DIM


def _split_heads(t, head0):
    zero = jnp.zeros_like(t)
    return jnp.concatenate([jnp.where(head0, t, zero), jnp.where(head0, zero, t)], axis=0)


def _rms_scale(x):
    return x * lax.rsqrt(jnp.mean(x * x, axis=-1, keepdims=True) + EPS)


PROJ_ROWS = 512


def _rms_proj_kernel(n_out, x_ref, *refs):
    g_refs, w_refs, o_refs = refs[:n_out], refs[n_out:2 * n_out], refs[2 * n_out:]
    xhat = _rms_scale(x_ref[...])
    for g_ref, w_ref, o_ref in zip(g_refs, w_refs, o_refs):
        xn = (xhat * g_ref[...]).astype(BF16)
        o_ref[...] = jnp.dot(xn, w_ref[...], preferred_element_type=F32).astype(o_ref.dtype)


def _rms_proj(x, gains, weights, tm):
    t, d = x.shape
    n_out = len(gains)
    in_specs = [pl.BlockSpec((tm, d), lambda i: (i, 0))]
    in_specs += [_resident((1, d)) for _ in gains]
    in_specs += [_resident(w.shape) for w in weights]
    out_specs = [pl.BlockSpec((tm, w.shape[1]), lambda i: (i, 0)) for w in weights]
    out_shape = [jax.ShapeDtypeStruct((t, w.shape[1]), BF16) for w in weights]
    return pl.pallas_call(
        functools.partial(_rms_proj_kernel, n_out),
        grid=(t // tm,),
        in_specs=in_specs,
        out_specs=out_specs,
        out_shape=out_shape,
        compiler_params=_params(("parallel",)),
        name="rms_proj",
    )(x, *[g.reshape(1, d) for g in gains], *weights)


SB_QBLKS = 4
SB_ROWS = SB_QBLKS * BLK
SB_KEY_UNROLL = 2


def _sb_kernel(q_ref, k_ref, v_ref, tri_ref, o_ref, carry_ref, acc_ref):
    qt = pl.program_id(2)
    head0 = _head0_lanes()
    tri = tri_ref[...]
    q_all = q_ref[...] * QK_SCALE

    def key_block(k0, row_lo, masked, carry, acc):
        kst = _split_heads(k_ref[pl.ds(k0, BLK), :], head0)
        vst = _split_heads(v_ref[pl.ds(k0, BLK), :], head0)
        z = lax.dot_general(q_all[row_lo:], kst, NT_DIMS, preferred_element_type=F32)
        sp = jnp.maximum(z, 0.0) + jnp.log(1.0 + jnp.exp2(jnp.abs(z) * (-LOG2E)))
        if masked:
            t_rel = lax.broadcasted_iota(jnp.int32, z.shape, 0)
            s_rel = lax.broadcasted_iota(jnp.int32, z.shape, 1) & (BLK - 1)
            causal = s_rel < t_rel
            sp = jnp.where(causal, sp, 0.0)
        hi = sp.astype(BF16)
        lo = (sp - hi.astype(F32)).astype(BF16)
        sums = [jnp.dot(jnp.concatenate([hi[:, h * BLK:(h + 1) * BLK], lo[:, h * BLK:(h + 1) * BLK]], axis=1),
                        tri, preferred_element_type=F32) for h in range(2)]
        suffix = jnp.concatenate([s[:, :BLK] for s in sums], axis=1)
        total = jnp.concatenate([s[:, BLK:] for s in sums], axis=1)
        w = jnp.exp2((z - (suffix + carry[row_lo:])) * LOG2E)
        if masked:
            w = jnp.where(causal, w, 0.0)
        pv = jnp.dot(w.astype(BF16), vst, preferred_element_type=F32)
        if row_lo:
            total = jnp.concatenate([jnp.zeros((row_lo, 2 * BLK), F32), total], axis=0)
            pv = jnp.concatenate([jnp.zeros((row_lo, LANES), F32), pv], axis=0)
        return carry + total, acc + pv

    carry = jnp.zeros((SB_ROWS, 2 * BLK), F32)
    acc = jnp.zeros((SB_ROWS, LANES), F32)
    r0 = qt * SB_ROWS
    for m in reversed(range(SB_QBLKS)):
        carry, acc = key_block(pl.multiple_of(r0 + m * BLK, BLK), m * BLK, True, carry, acc)
    carry_ref[...] = carry
    acc_ref[...] = acc

    def below_diagonal(i, _):
        carry, acc = carry_ref[...], acc_ref[...]
        for u in range(SB_KEY_UNROLL):
            k0 = pl.multiple_of(r0 - (i * SB_KEY_UNROLL + u + 1) * BLK, BLK)
            carry, acc = key_block(k0, 0, False, carry, acc)
        carry_ref[...] = carry
        acc_ref[...] = acc
        return 0

    lax.fori_loop(0, qt * (SB_QBLKS // SB_KEY_UNROLL), below_diagonal, 0)
    o_ref[...] = acc_ref[...].astype(o_ref.dtype)


def _sb_tri_weights():
    j = jnp.arange(2 * BLK)[:, None] % BLK
    c = jnp.arange(2 * BLK)[None, :]
    return ((c >= BLK) | (j >= c)).astype(BF16)


def _sb_attn(proj, batch, s_len):
    n_pairs = SB_WIDTH // LANES
    n_qt = s_len // SB_ROWS
    kv_blk = (s_len, LANES)
    return pl.pallas_call(
        _sb_kernel,
        grid=(batch, n_pairs, n_qt),
        in_specs=[
            pl.BlockSpec((SB_ROWS, LANES), lambda b, p, t: (b * n_qt + t, p)),
            pl.BlockSpec(kv_blk, lambda b, p, t: (b, n_pairs + p)),
            pl.BlockSpec(kv_blk, lambda b, p, t: (b, 2 * n_pairs + p)),
            _resident((2 * BLK, 2 * BLK)),
        ],
        out_specs=pl.BlockSpec((SB_ROWS, LANES), lambda b, p, t: (b * n_qt + t, p)),
        out_shape=jax.ShapeDtypeStruct((batch * s_len, SB_WIDTH), BF16),
        scratch_shapes=[pltpu.VMEM((SB_ROWS, 2 * BLK), F32), pltpu.VMEM((SB_ROWS, LANES), F32)],
        compiler_params=_params(("parallel", "parallel", "arbitrary")),
        name="sb_attn",
    )(proj, proj, proj, _sb_tri_weights())


MEM_ROWS = 256


def _mem_heads(q_ref, kv_ref, o_ref):
    s_len = q_ref.shape[0]
    mem_len = kv_ref.shape[0]
    head0 = _head0_lanes()
    for p in range(MEM_WIDTH // LANES):
        kst = _split_heads(kv_ref[:, p * LANES:(p + 1) * LANES], head0)
        vst = _split_heads(kv_ref[:, MEM_WIDTH + p * LANES:MEM_WIDTH + (p + 1) * LANES], head0)

        def step(c, carry):
            rows = pl.ds(pl.multiple_of(c * MEM_ROWS, MEM_ROWS), MEM_ROWS)
            q = q_ref[rows, p * LANES:(p + 1) * LANES] * QK_SCALE
            z = lax.dot_general(q, kst, NT_DIMS, preferred_element_type=F32)
            ps, dens = [], []
            for h in range(2):
                zh = z[:, h * mem_len:(h + 1) * mem_len]
                e = jnp.exp(zh - jnp.max(zh, axis=-1, keepdims=True))
                ps.append(e)
                dens.append(jnp.sum(e, axis=-1, keepdims=True))
            o = jnp.dot(jnp.concatenate(ps, axis=1).astype(BF16), vst, preferred_element_type=F32)
            o = o / jnp.where(head0, dens[0], dens[1])
            o_ref[rows, p * LANES:(p + 1) * LANES] = o.astype(o_ref.dtype)
            return carry

        lax.fori_loop(0, s_len // MEM_ROWS, step, 0)


def _mem_attn(proj, q_col_block, kv_mem, batch, s_len):
    mem_len = kv_mem.shape[0] // batch
    return pl.pallas_call(
        _mem_heads,
        grid=(batch,),
        in_specs=[
            pl.BlockSpec((s_len, MEM_WIDTH), lambda b: (b, q_col_block)),
            pl.BlockSpec((mem_len, 2 * MEM_WIDTH), lambda b: (b, 0)),
        ],
        out_specs=pl.BlockSpec((s_len, MEM_WIDTH), lambda b: (b, 0)),
        out_shape=jax.ShapeDtypeStruct((batch * s_len, MEM_WIDTH), BF16),
        compiler_params=_params(("parallel",)),
        name="mem_attn",
    )(proj, kv_mem)


def _proj_b_kernel(x_ref, gkv_ref, gq_ref, wq_ref, wkv_ref, o0, o1, o2, stage_ref):
    tm = x_ref.shape[0]
    s_len = o0.shape[0]
    c = pl.program_id(1)
    gw = GROUP_WIDTH
    xhat = _rms_scale(x_ref[...])
    q = jnp.dot((xhat * gq_ref[...]).astype(BF16), wq_ref[...], preferred_element_type=F32)
    kv = jnp.dot((xhat * gkv_ref[...]).astype(BF16), wkv_ref[...], preferred_element_type=F32)
    for g, ((_, d), o_ref) in enumerate(zip(DIL_GROUPS, (o0, o1, o2))):
        qkv = jnp.concatenate([q[:, g * gw:(g + 1) * gw], kv[:, g * gw:(g + 1) * gw],
                               kv[:, DIL_WIDTH + g * gw:DIL_WIDTH + (g + 1) * gw]], axis=1)
        if d == 1:
            rows = pl.ds(pl.multiple_of(c * tm, tm), tm)
            o_ref[rows, :3 * gw] = qkv.astype(BF16)
            o_ref[rows, 3 * gw:] = q[:, DIL_WIDTH:].astype(BF16)
        else:
            n_cols = 3 * gw // LANES
            for j in range(n_cols):
                stage_ref[j] = qkv[:, j * LANES:(j + 1) * LANES]
            n = tm // d
            for r in range(d):
                dst = pl.ds(pl.multiple_of(r * (s_len // d) + c * n, n), n)
                o_ref[dst, :] = jnp.concatenate(
                    [stage_ref[j, pl.ds(r, n, stride=d), :] for j in range(n_cols)], axis=1).astype(BF16)


def _proj_b(h, g_kv, g_q, w_kv, w_in, batch, s_len):
    d = h.shape[1]
    tm = PROJ_ROWS
    n_tiles = s_len // tm
    widths = [3 * GROUP_WIDTH + MEM_WIDTH, 3 * GROUP_WIDTH, 3 * GROUP_WIDTH]
    return pl.pallas_call(
        _proj_b_kernel,
        grid=(batch, n_tiles),
        in_specs=[pl.BlockSpec((tm, d), lambda b, c: (b * n_tiles + c, 0)), _resident((1, d)), _resident((1, d)),
                  _resident(w_in.shape), _resident(w_kv.shape)],
        out_specs=[pl.BlockSpec((s_len, n), lambda b, c: (b, 0)) for n in widths],
        out_shape=[jax.ShapeDtypeStruct((batch * s_len, n), BF16) for n in widths],
        scratch_shapes=[pltpu.VMEM((3 * GROUP_WIDTH // LANES, tm, LANES), F32)],
        compiler_params=_params(("parallel", "arbitrary")),
        name="proj_b",
    )(h, g_kv.reshape(1, d), g_q.reshape(1, d), w_in.astype(BF16), w_kv.astype(BF16))


COMBINE_ROWS = 256


def _softmax_parts(parts):
    m = parts[0]
    for t in parts[1:]:
        m = jnp.maximum(m, t)
    m = jnp.max(m, axis=-1, keepdims=True)
    es = [jnp.exp(t - m) for t in parts]
    tot = es[0]
    for e in es[1:]:
        tot = tot + e
    return es, jnp.sum(tot, axis=-1, keepdims=True), m


def _cross_attn_kernel(qkv0, qkv1, qkv2, kvm_ref, bias_first_ref, bias_ref, y0, y1, y2, om_ref,
                       o_cls, lse_cls, stage):
    s_len = qkv0.shape[0]
    head0 = _head0_lanes()
    gw = GROUP_WIDTH

    for g, ((_, d), qkv) in enumerate(zip(DIL_GROUPS, (qkv0, qkv1, qkv2))):
        class_rows = s_len // d
        n_blocks = class_rows // BLK
        for p in range(gw // LANES):
            qc = slice(p * LANES, (p + 1) * LANES)
            kc = slice(gw + p * LANES, gw + (p + 1) * LANES)
            vc = slice(2 * gw + p * LANES, 2 * gw + (p + 1) * LANES)

            def block(r, n, first):
                cur = pl.ds(pl.multiple_of(r * class_rows + n * BLK, BLK), BLK)
                q = qkv[cur, qc] * QK_SCALE
                kst = _split_heads(qkv[cur, kc], head0)
                vst = _split_heads(qkv[cur, vc], head0)
                if first:
                    z = lax.dot_general(q, kst, NT_DIMS, preferred_element_type=F32) + bias_first_ref[g, p]
                    heads = [[z[:, h * BLK:(h + 1) * BLK]] for h in range(2)]
                else:
                    prev = pl.ds(pl.multiple_of(r * class_rows + n * BLK - BLK, BLK), BLK)
                    kst = jnp.concatenate([_split_heads(qkv[prev, kc], head0), kst], axis=0)
                    vst = jnp.concatenate([_split_heads(qkv[prev, vc], head0), vst], axis=0)
                    z = lax.dot_general(q, kst, NT_DIMS, preferred_element_type=F32) + bias_ref[g, p]
                    heads = [[z[:, h * BLK:(h + 1) * BLK], z[:, (2 + h) * BLK:(3 + h) * BLK]] for h in range(2)]
                (e0, den0, m0), (e1, den1, m1) = [_softmax_parts(parts) for parts in heads]
                pmat = jnp.concatenate([x for pair in zip(e0, e1) for x in pair], axis=1).astype(BF16)
                o = jnp.dot(pmat, vst, preferred_element_type=F32)
                o_cls[g, cur, qc] = o / jnp.where(head0, den0, den1)
                lse_cls[g, cur, qc] = jnp.where(head0, m0 + jnp.log(den0), m1 + jnp.log(den1))

            def residue_class(r, carry):
                block(r, 0, True)
                if n_blocks > 1:
                    lax.fori_loop(1, n_blocks, lambda n, x: (block(r, n, False), x)[1], 0)
                return carry

            if d == 1:
                residue_class(0, 0)
            else:
                lax.fori_loop(0, d, residue_class, 0)

    def natural_rows(src, g, c, k):
        d = DIL_GROUPS[g][1]
        if d == 1:
            return src[g, pl.ds(pl.multiple_of(c * COMBINE_ROWS, COMBINE_ROWS), COMBINE_ROWS), :]
        n = COMBINE_ROWS // d
        n_cols = gw // LANES
        for r in range(d):
            rows = pl.ds(pl.multiple_of(r * (s_len // d) + c * n, n), n)
            for j in range(n_cols):
                stage[k * n_cols + j, pl.ds(r, n, stride=d), :] = src[g, rows, j * LANES:(j + 1) * LANES]
        return jnp.concatenate([stage[k * n_cols + j] for j in range(n_cols)], axis=1)

    def combine(c, carry):
        ls = [natural_rows(lse_cls, g, c, g) for g in range(3)]
        m = jnp.maximum(jnp.maximum(ls[0], ls[1]), ls[2])
        es = [jnp.exp(l - m) for l in ls]
        tot = es[0] + es[1] + es[2]
        rows = pl.ds(pl.multiple_of(c * COMBINE_ROWS, COMBINE_ROWS), COMBINE_ROWS)
        for g, y_ref in enumerate((y0, y1, y2)):
            y_ref[rows, :] = (natural_rows(o_cls, g, c, 3 + g) * (es[g] / tot)).astype(y_ref.dtype)
        return carry

    lax.fori_loop(0, s_len // COMBINE_ROWS, combine, 0)
    _mem_heads(qkv0.at[:, 3 * gw:], kvm_ref, om_ref)


def _dil_biases(slopes):
    i = jnp.arange(BLK)[:, None]
    j = jnp.arange(2 * BLK)[None, :]
    delta = i + BLK - j
    valid = (delta >= 0) & (delta <= BLK)
    firsts, others = [], []
    for g, (_, d) in enumerate(DIL_GROUPS):
        dist = (delta * d).astype(F32)
        f_p, o_p = [], []
        for p in range(2):
            halves = [jnp.where(valid, -slopes[g * HEADS_PER_GROUP + 2 * p + h] * dist, -jnp.inf) for h in range(2)]
            f_p.append(jnp.concatenate([halves[0][:, BLK:], halves[1][:, BLK:]], axis=1))
            o_p.append(jnp.concatenate([halves[0][:, :BLK], halves[1][:, :BLK],
                                        halves[0][:, BLK:], halves[1][:, BLK:]], axis=1))
        firsts.append(jnp.stack(f_p))
        others.append(jnp.stack(o_p))
    return jnp.stack(firsts), jnp.stack(others)


def _cross_attn(qkvs, kv_mem, batch, s_len):
    mem_len = kv_mem.shape[0] // batch
    slopes = 2.0 ** (-ALIBI_MAX_BIAS * jnp.arange(1, N_DIL_HEADS + 1, dtype=F32) / N_DIL_HEADS)
    bias_first, bias = _dil_biases(slopes)
    seq = lambda n: pl.BlockSpec((s_len, n), lambda b: (b, 0))
    n_out = len(DIL_GROUPS) + 1
    return pl.pallas_call(
        _cross_attn_kernel,
        grid=(batch,),
        in_specs=[seq(a.shape[1]) for a in qkvs]
        + [pl.BlockSpec((mem_len, 2 * MEM_WIDTH), lambda b: (b, 0)), _resident(bias_first.shape), _resident(bias.shape)],
        out_specs=[seq(GROUP_WIDTH)] * n_out,
        out_shape=[jax.ShapeDtypeStruct((batch * s_len, GROUP_WIDTH), BF16)] * n_out,
        scratch_shapes=[pltpu.VMEM((len(DIL_GROUPS), s_len, GROUP_WIDTH), F32)] * 2
        + [pltpu.VMEM((2 * len(DIL_GROUPS) * GROUP_WIDTH // LANES, COMBINE_ROWS, LANES), F32)],
        compiler_params=_params(("parallel",)),
        name="cross_attn",
    )(*qkvs, kv_mem, bias_first, bias)


FFN_ROWS = 512
FFN_CHUNK = 256
HALO = BF16_ROWS


def _out_ffn_kernel(n_attn, tiles_per_seq, final, h_ref, *refs):
    a_refs, wo_refs = refs[:n_attn], refs[n_attn:2 * n_attn]
    g_ref, wup_ref, wconv_ref, wdown_ref = refs[2 * n_attn:2 * n_attn + 4]
    rest = refs[2 * n_attn + 4:]
    if final:
        gf_ref, out_ref, xe_ref, act_ref = rest
    else:
        out_ref, xe_ref, act_ref = rest
    tm = h_ref.shape[0]
    i = pl.program_id(0)

    h = h_ref[...]
    for a_ref, wo_ref in zip(a_refs, wo_refs):
        h = h + jnp.dot(a_ref[...], wo_ref[...], preferred_element_type=F32)
    xn = _rms_scale(h) * g_ref[...]

    @pl.when(i % tiles_per_seq == 0)
    def _():
        xe_ref[:HALO, :] = jnp.zeros((HALO, xe_ref.shape[1]), xe_ref.dtype)

    @pl.when(i % tiles_per_seq != 0)
    def _():
        xe_ref[:HALO, :] = xe_ref[tm:, :]

    xe_ref[HALO:, :] = xn.astype(xe_ref.dtype)

    def conv(u, wc):
        return (wc[0:1] * pltpu.roll(u, 2, axis=0)[HALO:] + wc[1:2] * pltpu.roll(u, 1, axis=0)[HALO:]
                + wc[2:3] * u[HALO:])

    xe = xe_ref[...]
    for c in range(D_FF // FFN_CHUNK):
        ca_cols = slice(c * FFN_CHUNK, (c + 1) * FFN_CHUNK)
        cg_cols = slice(D_FF + c * FFN_CHUNK, D_FF + (c + 1) * FFN_CHUNK)
        ca = conv(jnp.dot(xe, wup_ref[:, ca_cols], preferred_element_type=F32), wconv_ref[:, ca_cols])
        cg = conv(jnp.dot(xe, wup_ref[:, cg_cols], preferred_element_type=F32), wconv_ref[:, cg_cols])
        act_ref[:, ca_cols] = (cg * jax.nn.sigmoid(cg) * ca).astype(act_ref.dtype)

    y = h + jnp.dot(act_ref[...], wdown_ref[...], preferred_element_type=F32)
    if final:
        y = _rms_scale(y) * gf_ref[...]
    out_ref[...] = y


def _out_ffn(h, attn_parts, w_out, g_ffn, w_up, w_conv, w_down, s_len, g_final=None):
    t, d = h.shape
    tm = FFN_ROWS
    n_attn = len(attn_parts)
    final = g_final is not None
    offs = [0]
    for a in attn_parts:
        offs.append(offs[-1] + a.shape[1])
    wo_parts = [w_out[offs[k]:offs[k + 1]].astype(BF16) for k in range(n_attn)]
    row = lambda n: pl.BlockSpec((tm, n), lambda i: (i, 0))
    in_specs = [row(d)] + [row(a.shape[1]) for a in attn_parts] + [_resident(w.shape) for w in wo_parts]
    in_specs += [_resident((1, d)), _resident(w_up.shape), _resident(w_conv.shape), _resident(w_down.shape)]
    args = [h, *attn_parts, *wo_parts, g_ffn.reshape(1, d), w_up.astype(BF16), w_conv, w_down.astype(BF16)]
    if final:
        in_specs.append(_resident((1, d)))
        args.append(g_final.reshape(1, d))
    return pl.pallas_call(
        functools.partial(_out_ffn_kernel, n_attn, s_len // tm, final),
        grid=(t // tm,),
        in_specs=in_specs,
        out_specs=row(d),
        out_shape=jax.ShapeDtypeStruct((t, d), F32),
        scratch_shapes=[pltpu.VMEM((HALO + tm, d), BF16), pltpu.VMEM((tm, D_FF), BF16)],
        compiler_params=_params(("arbitrary",)),
        name="out_ffn",
    )(*args)


def kernel(x, mem, a_norm_attn, a_w_in, a_w_out, a_norm_mem, a_w_mem_kv, a_norm_ffn, a_ffn_up, a_ffn_conv, a_ffn_down, kv_norm, w_kv_shared, b_norm_attn, b_w_in, b_w_out, b_norm_mem, b_w_mem_kv, b_norm_ffn, b_ffn_up, b_ffn_conv, b_ffn_down, final_norm):
    batch, s_len, d = x.shape
    assert a_w_in.shape[0] == 1 and b_w_in.shape[0] == 1, "one self-decoder and one cross-decoder layer"
    assert d == D_MODEL and s_len % (BLK * DIL_GROUPS[-1][1]) == 0 and s_len % FFN_ROWS == 0
    t = batch * s_len
    h = x.reshape(t, d)
    mem2 = mem.reshape(batch * mem.shape[1], d)

    kvm_a, kvm_b = _rms_proj(mem2, [a_norm_mem[0], b_norm_mem[0]],
                             [a_w_mem_kv[0].astype(BF16), b_w_mem_kv[0].astype(BF16)], PROJ_ROWS)

    (proj_a,) = _rms_proj(h, [a_norm_attn[0]], [a_w_in[0].astype(BF16)], PROJ_ROWS)
    o_sb = _sb_attn(proj_a, batch, s_len)
    o_mem = _mem_attn(proj_a, 3 * SB_WIDTH // MEM_WIDTH, kvm_a, batch, s_len)
    h = _out_ffn(h, [o_sb, o_mem], a_w_out[0], a_norm_ffn[0], a_ffn_up[0], a_ffn_conv[0], a_ffn_down[0], s_len)

    qkvs = _proj_b(h, kv_norm, b_norm_attn[0], w_kv_shared, b_w_in[0], batch, s_len)
    attn_parts = _cross_attn(qkvs, kvm_b, batch, s_len)
    h = _out_ffn(h, attn_parts, b_w_out[0], b_norm_ffn[0], b_ffn_up[0], b_ffn_conv[0], b_ffn_down[0], s_len,
                 g_final=final_norm)
    return h.reshape(batch, s_len, d)
```

```python
import functools

import jax
import jax.numpy as jnp
from jax import lax
from jax.experimental import pallas as pl
from jax.experimental.pallas import tpu as pltpu

D_MODEL = 1024
HEAD_DIM = 64
N_SB_HEADS = 12
N_MEM_HEADS = 4
DIL_GROUPS = ((128, 1), (512, 4), (2048, 16))
HEADS_PER_GROUP = 4
N_DIL_HEADS = HEADS_PER_GROUP * len(DIL_GROUPS)
SB_WIDTH = N_SB_HEADS * HEAD_DIM
MEM_WIDTH = N_MEM_HEADS * HEAD_DIM
DIL_WIDTH = N_DIL_HEADS * HEAD_DIM
GROUP_WIDTH = HEADS_PER_GROUP * HEAD_DIM
D_FF = 2816
CONV_WIDTH = 3
EPS = 1e-6
ALIBI_MAX_BIAS = 8.0
QK_SCALE = HEAD_DIM ** -0.5
LOG2E = 1.4426950408889634

LANES = 128
BF16_ROWS = 16
VMEM_LIMIT_BYTES = 56 * 1024 * 1024

BLK = 128
F32 = jnp.float32
BF16 = jnp.bfloat16
NT_DIMS = (((1,), (1,)), ((), ()))


def _params(semantics):
    return pltpu.CompilerParams(dimension_semantics=semantics, vmem_limit_bytes=VMEM_LIMIT_BYTES)


def _resident(shape):
    return pl.BlockSpec(shape, lambda *_: (0,) * len(shape), pipeline_mode=pl.Buffered(1))


def _head0_lanes():
    return lax.broadcasted_iota(jnp.int32, (1, LANES), 1) < HEAD_DIM


def _split_heads(t, head0):
    zero = jnp.zeros_like(t)
    return jnp.concatenate([jnp.where(head0, t, zero), jnp.where(head0, zero, t)], axis=0)


def _with_den_cols(vst):
    row = lax.broadcasted_iota(jnp.int32, vst.shape, 0)
    lane = lax.broadcasted_iota(jnp.int32, vst.shape, 1)
    owns = (row < vst.shape[0] // 2) == (lane < HEAD_DIM)
    return jnp.concatenate([vst, jnp.where(owns, 1.0, 0.0).astype(vst.dtype)], axis=1)


def _rms_scale(x):
    return x * lax.rsqrt(jnp.mean(x * x, axis=-1, keepdims=True) + EPS)


PROJ_ROWS = 512


def _rms_proj_kernel(n_out, x_ref, *refs):
    g_refs, w_refs, o_refs = refs[:n_out], refs[n_out:2 * n_out], refs[2 * n_out:]
    xhat = _rms_scale(x_ref[...])
    for g_ref, w_ref, o_ref in zip(g_refs, w_refs, o_refs):
        xn = (xhat * g_ref[...]).astype(BF16)
        o_ref[...] = jnp.dot(xn, w_ref[...], preferred_element_type=F32).astype(o_ref.dtype)


def _rms_proj(x, gains, weights, tm):
    t, d = x.shape
    n_out = len(gains)
    in_specs = [pl.BlockSpec((tm, d), lambda i: (i, 0))]
    in_specs += [_resident((1, d)) for _ in gains]
    in_specs += [_resident(w.shape) for w in weights]
    out_specs = [pl.BlockSpec((tm, w.shape[1]), lambda i: (i, 0)) for w in weights]
    out_shape = [jax.ShapeDtypeStruct((t, w.shape[1]), BF16) for w in weights]
    return pl.pallas_call(
        functools.partial(_rms_proj_kernel, n_out),
        grid=(t // tm,),
        in_specs=in_specs,
        out_specs=out_specs,
        out_shape=out_shape,
        compiler_params=_params(("parallel",)),
        name="rms_proj",
    )(x, *[g.reshape(1, d) for g in gains], *weights)


SB_QBLKS = 4
SB_ROWS = SB_QBLKS * BLK
SB_KEY_UNROLL = 2
SB_UNDERFLOW = 106.0


def _sb_kernel(q_ref, k_ref, v_ref, tri_ref, o_ref, carry_ref, acc_ref):
    qt = pl.program_id(2)
    head0 = _head0_lanes()
    tri = tri_ref[...]
    q_all = q_ref[...] * QK_SCALE

    def key_block(k0, row_lo, masked, carry, acc):
        kst = _split_heads(k_ref[pl.ds(k0, BLK), :], head0)
        vst = _split_heads(v_ref[pl.ds(k0, BLK), :], head0)
        z = lax.dot_general(q_all[row_lo:], kst, NT_DIMS, preferred_element_type=F32)
        sp = jnp.maximum(z, 0.0) + jnp.log(1.0 + jnp.exp2(jnp.abs(z) * (-LOG2E)))
        if masked:
            t_rel = lax.broadcasted_iota(jnp.int32, z.shape, 0)
            s_rel = lax.broadcasted_iota(jnp.int32, z.shape, 1) & (BLK - 1)
            causal = s_rel < t_rel
            sp = jnp.where(causal, sp, 0.0)
        hi = sp.astype(BF16)
        lo = (sp - hi.astype(F32)).astype(BF16)
        sums = [jnp.dot(jnp.concatenate([hi[:, h * BLK:(h + 1) * BLK], lo[:, h * BLK:(h + 1) * BLK]], axis=1),
                        tri, preferred_element_type=F32) for h in range(2)]
        suffix = jnp.concatenate([s[:, :BLK] for s in sums], axis=1)
        total = jnp.concatenate([s[:, BLK:] for s in sums], axis=1)
        w = jnp.exp2((z - (suffix + carry[row_lo:])) * LOG2E)
        if masked:
            w = jnp.where(causal, w, 0.0)
        pv = jnp.dot(w.astype(BF16), vst, preferred_element_type=F32)
        if row_lo:
            total = jnp.concatenate([jnp.zeros((row_lo, 2 * BLK), F32), total], axis=0)
            pv = jnp.concatenate([jnp.zeros((row_lo, LANES), F32), pv], axis=0)
        return carry + total, acc + pv

    carry = jnp.zeros((SB_ROWS, 2 * BLK), F32)
    acc = jnp.zeros((SB_ROWS, LANES), F32)
    r0 = qt * SB_ROWS
    for m in reversed(range(SB_QBLKS)):
        carry, acc = key_block(pl.multiple_of(r0 + m * BLK, BLK), m * BLK, True, carry, acc)
    carry_ref[...] = carry
    acc_ref[...] = acc

    def below_diagonal(state):
        i, _ = state
        carry, acc = carry_ref[...], acc_ref[...]
        for u in range(SB_KEY_UNROLL):
            k0 = pl.multiple_of(r0 - (i * SB_KEY_UNROLL + u + 1) * BLK, BLK)
            carry, acc = key_block(k0, 0, False, carry, acc)
        carry_ref[...] = carry
        acc_ref[...] = acc
        return i + 1, jnp.min(carry)

    n_iters = qt * (SB_QBLKS // SB_KEY_UNROLL)
    lax.while_loop(lambda s: jnp.logical_and(s[0] < n_iters, s[1] < SB_UNDERFLOW), below_diagonal,
                   (jnp.int32(0), jnp.min(carry)))
    o_ref[...] = acc_ref[...].astype(o_ref.dtype)


def _sb_tri_weights():
    j = jnp.arange(2 * BLK)[:, None] % BLK
    c = jnp.arange(2 * BLK)[None, :]
    return ((c >= BLK) | (j >= c)).astype(BF16)


def _sb_attn(proj, batch, s_len):
    n_pairs = SB_WIDTH // LANES
    n_qt = s_len // SB_ROWS
    kv_blk = (s_len, LANES)
    return pl.pallas_call(
        _sb_kernel,
        grid=(batch, n_pairs, n_qt),
        in_specs=[
            pl.BlockSpec((SB_ROWS, LANES), lambda b, p, t: (b * n_qt + t, p)),
            pl.BlockSpec(kv_blk, lambda b, p, t: (b, n_pairs + p)),
            pl.BlockSpec(kv_blk, lambda b, p, t: (b, 2 * n_pairs + p)),
            _resident((2 * BLK, 2 * BLK)),
        ],
        out_specs=pl.BlockSpec((SB_ROWS, LANES), lambda b, p, t: (b * n_qt + t, p)),
        out_shape=jax.ShapeDtypeStruct((batch * s_len, SB_WIDTH), BF16),
        scratch_shapes=[pltpu.VMEM((SB_ROWS, 2 * BLK), F32), pltpu.VMEM((SB_ROWS, LANES), F32)],
        compiler_params=_params(("parallel", "parallel", "arbitrary")),
        name="sb_attn",
    )(proj, proj, proj, _sb_tri_weights())


MEM_ROWS = 256


def _mem_heads(q_ref, kv_ref, o_ref):
    s_len = q_ref.shape[0]
    mem_len = kv_ref.shape[0]
    head0 = _head0_lanes()
    n_pairs = MEM_WIDTH // LANES
    ksts = [_split_heads(kv_ref[:, p * LANES:(p + 1) * LANES], head0) for p in range(n_pairs)]
    vsts = [_with_den_cols(_split_heads(kv_ref[:, MEM_WIDTH + p * LANES:MEM_WIDTH + (p + 1) * LANES], head0))
            for p in range(n_pairs)]

    def step(c, carry):
        rows = pl.ds(pl.multiple_of(c * MEM_ROWS, MEM_ROWS), MEM_ROWS)
        for p in range(n_pairs):
            q = q_ref[rows, p * LANES:(p + 1) * LANES] * QK_SCALE
            z = lax.dot_general(q, ksts[p], NT_DIMS, preferred_element_type=F32)
            es = []
            for h in range(2):
                zh = z[:, h * mem_len:(h + 1) * mem_len]
                es.append(jnp.exp(zh - jnp.max(zh, axis=-1, keepdims=True)))
            od = jnp.dot(jnp.concatenate(es, axis=1).astype(BF16), vsts[p], preferred_element_type=F32)
            o_ref[rows, p * LANES:(p + 1) * LANES] = (od[:, :LANES] / od[:, LANES:]).astype(o_ref.dtype)
        return carry

    lax.fori_loop(0, s_len // MEM_ROWS, step, 0)


def _mem_attn(proj, q_col_block, kv_mem, batch, s_len):
    mem_len = kv_mem.shape[0] // batch
    return pl.pallas_call(
        _mem_heads,
        grid=(batch,),
        in_specs=[
            pl.BlockSpec((s_len, MEM_WIDTH), lambda b: (b, q_col_block)),
            pl.BlockSpec((mem_len, 2 * MEM_WIDTH), lambda b: (b, 0)),
        ],
        out_specs=pl.BlockSpec((s_len, MEM_WIDTH), lambda b: (b, 0)),
        out_shape=jax.ShapeDtypeStruct((batch * s_len, MEM_WIDTH), BF16),
        compiler_params=_params(("parallel",)),
        name="mem_attn",
    )(proj, kv_mem)


def _proj_b_kernel(x_ref, gkv_ref, gq_ref, wq_ref, wkv_ref, o0, o1, o2, stage_ref):
    tm = x_ref.shape[0]
    s_len = o0.shape[0]
    c = pl.program_id(1)
    gw = GROUP_WIDTH
    xhat = _rms_scale(x_ref[...])
    q = jnp.dot((xhat * gq_ref[...]).astype(BF16), wq_ref[...], preferred_element_type=F32)
    kv = jnp.dot((xhat * gkv_ref[...]).astype(BF16), wkv_ref[...], preferred_element_type=F32)
    for g, ((_, d), o_ref) in enumerate(zip(DIL_GROUPS, (o0, o1, o2))):
        qkv = jnp.concatenate([q[:, g * gw:(g + 1) * gw], kv[:, g * gw:(g + 1) * gw],
                               kv[:, DIL_WIDTH + g * gw:DIL_WIDTH + (g + 1) * gw]], axis=1)
        if d == 1:
            rows = pl.ds(pl.multiple_of(c * tm, tm), tm)
            o_ref[rows, :3 * gw] = qkv.astype(BF16)
            o_ref[rows, 3 * gw:] = q[:, DIL_WIDTH:].astype(BF16)
        else:
            n_cols = 3 * gw // LANES
            for j in range(n_cols):
                stage_ref[j] = qkv[:, j * LANES:(j + 1) * LANES]
            n = tm // d
            for r in range(d):
                dst = pl.ds(pl.multiple_of(r * (s_len // d) + c * n, n), n)
                o_ref[dst, :] = jnp.concatenate(
                    [stage_ref[j, pl.ds(r, n, stride=d), :] for j in range(n_cols)], axis=1).astype(BF16)


def _proj_b(h, g_kv, g_q, w_kv, w_in, batch, s_len):
    d = h.shape[1]
    tm = PROJ_ROWS
    n_tiles = s_len // tm
    widths = [3 * GROUP_WIDTH + MEM_WIDTH, 3 * GROUP_WIDTH, 3 * GROUP_WIDTH]
    return pl.pallas_call(
        _proj_b_kernel,
        grid=(batch, n_tiles),
        in_specs=[pl.BlockSpec((tm, d), lambda b, c: (b * n_tiles + c, 0)), _resident((1, d)), _resident((1, d)),
                  _resident(w_in.shape), _resident(w_kv.shape)],
        out_specs=[pl.BlockSpec((s_len, n), lambda b, c: (b, 0)) for n in widths],
        out_shape=[jax.ShapeDtypeStruct((batch * s_len, n), BF16) for n in widths],
        scratch_shapes=[pltpu.VMEM((3 * GROUP_WIDTH // LANES, tm, LANES), F32)],
        compiler_params=_params(("parallel", "arbitrary")),
        name="proj_b",
    )(h, g_kv.reshape(1, d), g_q.reshape(1, d), w_in.astype(BF16), w_kv.astype(BF16))


COMBINE_ROWS = 256


DIL_BLOCKS_PER_ITER = 4


def _softmax_parts(parts):
    m = parts[0]
    for t in parts[1:]:
        m = jnp.maximum(m, t)
    m = jnp.max(m, axis=-1, keepdims=True)
    return [jnp.exp(t - m) for t in parts], m


def _cross_attn_kernel(qkv0, qkv1, qkv2, kvm_ref, bias_first_ref, bias_ref, y0, y1, y2, om_ref,
                       o_cls, lse_cls, stage):
    s_len = qkv0.shape[0]
    head0 = _head0_lanes()
    gw = GROUP_WIDTH

    def block(g, qkv, p, row0, first):
        qc = slice(p * LANES, (p + 1) * LANES)
        kc = slice(gw + p * LANES, gw + (p + 1) * LANES)
        vc = slice(2 * gw + p * LANES, 2 * gw + (p + 1) * LANES)
        cur = pl.ds(pl.multiple_of(row0, BLK), BLK)
        q = qkv[cur, qc] * QK_SCALE
        kst = _split_heads(qkv[cur, kc], head0)
        vst = _with_den_cols(_split_heads(qkv[cur, vc], head0))
        if first:
            z = lax.dot_general(q, kst, NT_DIMS, preferred_element_type=F32) + bias_first_ref[g, p]
            heads = [[z[:, h * BLK:(h + 1) * BLK]] for h in range(2)]
        else:
            prev = pl.ds(pl.multiple_of(row0 - BLK, BLK), BLK)
            kst = jnp.concatenate([_split_heads(qkv[prev, kc], head0), kst], axis=0)
            vst = jnp.concatenate([_with_den_cols(_split_heads(qkv[prev, vc], head0)), vst], axis=0)
            z = lax.dot_general(q, kst, NT_DIMS, preferred_element_type=F32) + bias_ref[g, p]
            heads = [[z[:, h * BLK:(h + 1) * BLK], z[:, (2 + h) * BLK:(3 + h) * BLK]] for h in range(2)]
        (e0, m0), (e1, m1) = [_softmax_parts(parts) for parts in heads]
        pmat = jnp.concatenate([x for pair in zip(e0, e1) for x in pair], axis=1).astype(BF16)
        od = jnp.dot(pmat, vst, preferred_element_type=F32)
        o_cls[g, cur, qc] = od[:, :LANES] / od[:, LANES:]
        lse_cls[g, cur, qc] = jnp.where(head0, m0, m1) + jnp.log(od[:, LANES:])

    for g, ((_, d), qkv) in enumerate(zip(DIL_GROUPS, (qkv0, qkv1, qkv2))):
        blocks_per_class = s_len // d // BLK
        n_iters = s_len // BLK // DIL_BLOCKS_PER_ITER

        def blocks(it, starts_class, g=g, qkv=qkv, blocks_per_class=blocks_per_class):
            for k in range(DIL_BLOCKS_PER_ITER):
                first = blocks_per_class == 1 or (k == 0 and starts_class)
                for p in range(gw // LANES):
                    block(g, qkv, p, (it * DIL_BLOCKS_PER_ITER + k) * BLK, first)

        if blocks_per_class > DIL_BLOCKS_PER_ITER:
            blocks(0, True)
            lax.fori_loop(1, n_iters, lambda it, x, blocks=blocks: (blocks(it, False), x)[1], 0)
        else:
            lax.fori_loop(0, n_iters, lambda it, x, blocks=blocks: (blocks(it, True), x)[1], 0)

    def natural_rows(src, g, c, k):
        d = DIL_GROUPS[g][1]
        if d == 1:
            return src[g, pl.ds(pl.multiple_of(c * COMBINE_ROWS, COMBINE_ROWS), COMBINE_ROWS), :]
        n = COMBINE_ROWS // d
        n_cols = gw // LANES
        for r in range(d):
            rows = pl.ds(pl.multiple_of(r * (s_len // d) + c * n, n), n)
            for j in range(n_cols):
                stage[k * n_cols + j, pl.ds(r, n, stride=d), :] = src[g, rows, j * LANES:(j + 1) * LANES]
        return jnp.concatenate([stage[k * n_cols + j] for j in range(n_cols)], axis=1)

    def combine(c, carry):
        ls = [natural_rows(lse_cls, g, c, g) for g in range(3)]
        m = jnp.maximum(jnp.maximum(ls[0], ls[1]), ls[2])
        es = [jnp.exp(l - m) for l in ls]
        tot = es[0] + es[1] + es[2]
        rows = pl.ds(pl.multiple_of(c * COMBINE_ROWS, COMBINE_ROWS), COMBINE_ROWS)
        for g, y_ref in enumerate((y0, y1, y2)):
            y_ref[rows, :] = (natural_rows(o_cls, g, c, 3 + g) * (es[g] / tot)).astype(y_ref.dtype)
        return carry

    lax.fori_loop(0, s_len // COMBINE_ROWS, combine, 0)
    _mem_heads(qkv0.at[:, 3 * gw:], kvm_ref, om_ref)


def _dil_biases(slopes):
    i = jnp.arange(BLK)[:, None]
    j = jnp.arange(2 * BLK)[None, :]
    delta = i + BLK - j
    valid = (delta >= 0) & (delta <= BLK)
    firsts, others = [], []
    for g, (_, d) in enumerate(DIL_GROUPS):
        dist = (delta * d).astype(F32)
        f_p, o_p = [], []
        for p in range(2):
            halves = [jnp.where(valid, -slopes[g * HEADS_PER_GROUP + 2 * p + h] * dist, -jnp.inf) for h in range(2)]
            f_p.append(jnp.concatenate([halves[0][:, BLK:], halves[1][:, BLK:]], axis=1))
            o_p.append(jnp.concatenate([halves[0][:, :BLK], halves[1][:, :BLK],
                                        halves[0][:, BLK:], halves[1][:, BLK:]], axis=1))
        firsts.append(jnp.stack(f_p))
        others.append(jnp.stack(o_p))
    return jnp.stack(firsts), jnp.stack(others)


def _cross_attn(qkvs, kv_mem, batch, s_len):
    mem_len = kv_mem.shape[0] // batch
    slopes = 2.0 ** (-ALIBI_MAX_BIAS * jnp.arange(1, N_DIL_HEADS + 1, dtype=F32) / N_DIL_HEADS)
    bias_first, bias = _dil_biases(slopes)
    seq = lambda n: pl.BlockSpec((s_len, n), lambda b: (b, 0))
    n_out = len(DIL_GROUPS) + 1
    return pl.pallas_call(
        _cross_attn_kernel,
        grid=(batch,),
        in_specs=[seq(a.shape[1]) for a in qkvs]
        + [pl.BlockSpec((mem_len, 2 * MEM_WIDTH), lambda b: (b, 0)), _resident(bias_first.shape), _resident(bias.shape)],
        out_specs=[seq(GROUP_WIDTH)] * n_out,
        out_shape=[jax.ShapeDtypeStruct((batch * s_len, GROUP_WIDTH), BF16)] * n_out,
        scratch_shapes=[pltpu.VMEM((len(DIL_GROUPS), s_len, GROUP_WIDTH), F32)] * 2
        + [pltpu.VMEM((2 * len(DIL_GROUPS) * GROUP_WIDTH // LANES, COMBINE_ROWS, LANES), F32)],
        compiler_params=_params(("parallel",)),
        name="cross_attn",
    )(*qkvs, kv_mem, bias_first, bias)


FFN_ROWS = 512
FFN_CHUNK = 256
HALO = BF16_ROWS


def _out_ffn_kernel(n_attn, tiles_per_seq, final, h_ref, *refs):
    a_refs, wo_refs = refs[:n_attn], refs[n_attn:2 * n_attn]
    g_ref, wup_ref, wconv_ref, wdown_ref = refs[2 * n_attn:2 * n_attn + 4]
    rest = refs[2 * n_attn + 4:]
    if final:
        gf_ref, out_ref, xe_ref, act_ref = rest
    else:
        out_ref, xe_ref, act_ref = rest
    tm = h_ref.shape[0]
    i = pl.program_id(0)

    h = h_ref[...]
    for a_ref, wo_ref in zip(a_refs, wo_refs):
        h = h + jnp.dot(a_ref[...], wo_ref[...], preferred_element_type=F32)
    xn = _rms_scale(h) * g_ref[...]

    @pl.when(i % tiles_per_seq == 0)
    def _():
        xe_ref[:HALO, :] = jnp.zeros((HALO, xe_ref.shape[1]), xe_ref.dtype)

    @pl.when(i % tiles_per_seq != 0)
    def _():
        xe_ref[:HALO, :] = xe_ref[tm:, :]

    xe_ref[HALO:, :] = xn.astype(xe_ref.dtype)

    def conv(u, wc):
        return (wc[0:1] * pltpu.roll(u, 2, axis=0)[HALO:] + wc[1:2] * pltpu.roll(u, 1, axis=0)[HALO:]
                + wc[2:3] * u[HALO:])

    xe = xe_ref[...]
    for c in range(D_FF // FFN_CHUNK):
        ca_cols = slice(c * FFN_CHUNK, (c + 1) * FFN_CHUNK)
        cg_cols = slice(D_FF + c * FFN_CHUNK, D_FF + (c + 1) * FFN_CHUNK)
        ca = conv(jnp.dot(xe, wup_ref[:, ca_cols], preferred_element_type=F32), wconv_ref[:, ca_cols])
        cg = conv(jnp.dot(xe, wup_ref[:, cg_cols], preferred_element_type=F32), wconv_ref[:, cg_cols])
        act_ref[:, ca_cols] = (cg * jax.nn.sigmoid(cg) * ca).astype(act_ref.dtype)

    y = h + jnp.dot(act_ref[...], wdown_ref[...], preferred_element_type=F32)
    if final:
        y = _rms_scale(y) * gf_ref[...]
    out_ref[...] = y


def _out_ffn(h, attn_parts, w_out, g_ffn, w_up, w_conv, w_down, s_len, g_final=None):
    t, d = h.shape
    tm = FFN_ROWS
    n_attn = len(attn_parts)
    final = g_final is not None
    offs = [0]
    for a in attn_parts:
        offs.append(offs[-1] + a.shape[1])
    wo_parts = [w_out[offs[k]:offs[k + 1]].astype(BF16) for k in range(n_attn)]
    row = lambda n: pl.BlockSpec((tm, n), lambda i: (i, 0))
    in_specs = [row(d)] + [row(a.shape[1]) for a in attn_parts] + [_resident(w.shape) for w in wo_parts]
    in_specs += [_resident((1, d)), _resident(w_up.shape), _resident(w_conv.shape), _resident(w_down.shape)]
    args = [h, *attn_parts, *wo_parts, g_ffn.reshape(1, d), w_up.astype(BF16), w_conv, w_down.astype(BF16)]
    if final:
        in_specs.append(_resident((1, d)))
        args.append(g_final.reshape(1, d))
    return pl.pallas_call(
        functools.partial(_out_ffn_kernel, n_attn, s_len // tm, final),
        grid=(t // tm,),
        in_specs=in_specs,
        out_specs=row(d),
        out_shape=jax.ShapeDtypeStruct((t, d), F32),
        scratch_shapes=[pltpu.VMEM((HALO + tm, d), BF16), pltpu.VMEM((tm, D_FF), BF16)],
        compiler_params=_params(("arbitrary",)),
        name="out_ffn",
    )(*args)


def kernel(x, mem, a_norm_attn, a_w_in, a_w_out, a_norm_mem, a_w_mem_kv, a_norm_ffn, a_ffn_up, a_ffn_conv, a_ffn_down, kv_norm, w_kv_shared, b_norm_attn, b_w_in, b_w_out, b_norm_mem, b_w_mem_kv, b_norm_ffn, b_ffn_up, b_ffn_conv, b_ffn_down, final_norm):
    batch, s_len, d = x.shape
    assert a_w_in.shape[0] == 1 and b_w_in.shape[0] == 1, "one self-decoder and one cross-decoder layer"
    assert d == D_MODEL and s_len % (BLK * DIL_GROUPS[-1][1]) == 0 and s_len % FFN_ROWS == 0
    t = batch * s_len
    h = x.reshape(t, d)
    mem2 = mem.reshape(batch * mem.shape[1], d)

    kvm_a, kvm_b = _rms_proj(mem2, [a_norm_mem[0], b_norm_mem[0]],
                             [a_w_mem_kv[0].astype(BF16), b_w_mem_kv[0].astype(BF16)], PROJ_ROWS)

    (proj_a,) = _rms_proj(h, [a_norm_attn[0]], [a_w_in[0].astype(BF16)], PROJ_ROWS)
    o_sb = _sb_attn(proj_a, batch, s_len)
    o_mem = _mem_attn(proj_a, 3 * SB_WIDTH // MEM_WIDTH, kvm_a, batch, s_len)
    h = _out_ffn(h, [o_sb, o_mem], a_w_out[0], a_norm_ffn[0], a_ffn_up[0], a_ffn_conv[0], a_ffn_down[0], s_len)

    qkvs = _proj_b(h, kv_norm, b_norm_attn[0], w_kv_shared, b_w_in[0], batch, s_len)
    attn_parts = _cross_attn(qkvs, kvm_b, batch, s_len)
    h = _out_ffn(h, attn_parts, b_w_out[0], b_norm_ffn[0], b_ffn_up[0], b_ffn_conv[0], b_ffn_down[0], s_len,
                 g_final=final_norm)
    return h.reshape(batch, s_len, d)
```

```python
import functools

import jax
import jax.numpy as jnp
from jax import lax
from jax.experimental import pallas as pl
from jax.experimental.pallas import tpu as pltpu

D_MODEL = 1024
HEAD_DIM = 64
N_SB_HEADS = 12
N_MEM_HEADS = 4
DIL_GROUPS = ((128, 1), (512, 4), (2048, 16))
HEADS_PER_GROUP = 4
N_DIL_HEADS = HEADS_PER_GROUP * len(DIL_GROUPS)
SB_WIDTH = N_SB_HEADS * HEAD_DIM
MEM_WIDTH = N_MEM_HEADS * HEAD_DIM
DIL_WIDTH = N_DIL_HEADS * HEAD_DIM
GROUP_WIDTH = HEADS_PER_GROUP * HEAD_DIM
D_FF = 2816
CONV_WIDTH = 3
EPS = 1e-6
ALIBI_MAX_BIAS = 8.0
QK_SCALE = HEAD_DIM ** -0.5
LOG2E = 1.4426950408889634

LANES = 128
BF16_ROWS = 16
VMEM_LIMIT_BYTES = 56 * 1024 * 1024

BLK = 128
F32 = jnp.float32
BF16 = jnp.bfloat16
NT_DIMS = (((1,), (1,)), ((), ()))


def _params(semantics):
    return pltpu.CompilerParams(dimension_semantics=semantics, vmem_limit_bytes=VMEM_LIMIT_BYTES)


def _resident(shape):
    return pl.BlockSpec(shape, lambda *_: (0,) * len(shape), pipeline_mode=pl.Buffered(1))


def _head0_lanes():
    return lax.broadcasted_iota(jnp.int32, (1, LANES), 1) < HEAD_DIM


def _split_heads(t, head0):
    zero = jnp.zeros_like(t)
    return jnp.concatenate([jnp.where(head0, t, zero), jnp.where(head0, zero, t)], axis=0)


def _with_den_cols(vst):
    row = lax.broadcasted_iota(jnp.int32, vst.shape, 0)
    lane = lax.broadcasted_iota(jnp.int32, vst.shape, 1)
    owns = (row < vst.shape[0] // 2) == (lane < HEAD_DIM)
    return jnp.concatenate([vst, jnp.where(owns, 1.0, 0.0).astype(vst.dtype)], axis=1)


def _rms_scale(x):
    return x * lax.rsqrt(jnp.mean(x * x, axis=-1, keepdims=True) + EPS)


PROJ_ROWS = 512


def _rms_proj_kernel(n_out, x_ref, *refs):
    g_refs, w_refs, o_refs = refs[:n_out], refs[n_out:2 * n_out], refs[2 * n_out:]
    xhat = _rms_scale(x_ref[...])
    for g_ref, w_ref, o_ref in zip(g_refs, w_refs, o_refs):
        xn = (xhat * g_ref[...]).astype(BF16)
        o_ref[...] = jnp.dot(xn, w_ref[...], preferred_element_type=F32).astype(o_ref.dtype)


def _rms_proj(x, gains, weights, tm):
    t, d = x.shape
    n_out = len(gains)
    in_specs = [pl.BlockSpec((tm, d), lambda i: (i, 0))]
    in_specs += [_resident((1, d)) for _ in gains]
    in_specs += [_resident(w.shape) for w in weights]
    out_specs = [pl.BlockSpec((tm, w.shape[1]), lambda i: (i, 0)) for w in weights]
    out_shape = [jax.ShapeDtypeStruct((t, w.shape[1]), BF16) for w in weights]
    return pl.pallas_call(
        functools.partial(_rms_proj_kernel, n_out),
        grid=(t // tm,),
        in_specs=in_specs,
        out_specs=out_specs,
        out_shape=out_shape,
        compiler_params=_params(("parallel",)),
        name="rms_proj",
    )(x, *[g.reshape(1, d) for g in gains], *weights)


SB_QBLKS = 4
SB_ROWS = SB_QBLKS * BLK
SB_KEY_UNROLL = 2
SB_UNDERFLOW = 106.0


def _sb_kernel(q_ref, k_ref, v_ref, tri_ref, o_ref, carry_ref, acc_ref):
    qt = pl.program_id(2)
    head0 = _head0_lanes()
    tri = tri_ref[...]
    q_all = q_ref[...] * QK_SCALE

    def key_block(k0, row_lo, masked, carry, acc):
        kst = _split_heads(k_ref[pl.ds(k0, BLK), :], head0)
        vst = _split_heads(v_ref[pl.ds(k0, BLK), :], head0)
        z = lax.dot_general(q_all[row_lo:], kst, NT_DIMS, preferred_element_type=F32)
        sp = jnp.maximum(z, 0.0) + jnp.log(1.0 + jnp.exp2(jnp.abs(z) * (-LOG2E)))
        if masked:
            t_rel = lax.broadcasted_iota(jnp.int32, (BLK, 2 * BLK), 0)
            s_rel = lax.broadcasted_iota(jnp.int32, (BLK, 2 * BLK), 1) & (BLK - 1)
            causal = s_rel < t_rel

            def mask_diagonal(t):
                top = jnp.where(causal, t[:BLK], 0.0)
                return top if t.shape[0] == BLK else jnp.concatenate([top, t[BLK:]], axis=0)

            sp = mask_diagonal(sp)
        hi = sp.astype(BF16)
        lo = (sp - hi.astype(F32)).astype(BF16)
        sums = [jnp.dot(jnp.concatenate([hi[:, h * BLK:(h + 1) * BLK], lo[:, h * BLK:(h + 1) * BLK]], axis=1),
                        tri, preferred_element_type=F32) for h in range(2)]
        suffix = jnp.concatenate([s[:, :BLK] for s in sums], axis=1)
        total = jnp.concatenate([s[:, BLK:] for s in sums], axis=1)
        w = jnp.exp2((z - (suffix + carry[row_lo:])) * LOG2E)
        if masked:
            w = mask_diagonal(w)
        pv = jnp.dot(w.astype(BF16), vst, preferred_element_type=F32)
        if row_lo:
            total = jnp.concatenate([jnp.zeros((row_lo, 2 * BLK), F32), total], axis=0)
            pv = jnp.concatenate([jnp.zeros((row_lo, LANES), F32), pv], axis=0)
        return carry + total, acc + pv

    r0 = qt * SB_ROWS

    def diagonal_blocks():
        carry = jnp.zeros((SB_ROWS, 2 * BLK), F32)
        acc = jnp.zeros((SB_ROWS, LANES), F32)
        for m in reversed(range(SB_QBLKS)):
            carry, acc = key_block(pl.multiple_of(r0 + m * BLK, BLK), m * BLK, True, carry, acc)
        return carry, acc

    def below_diagonal(i, carry, acc):
        for u in range(SB_KEY_UNROLL):
            k0 = pl.multiple_of(r0 - (i * SB_KEY_UNROLL + u + 1) * BLK, BLK)
            carry, acc = key_block(k0, 0, False, carry, acc)
        return carry, acc

    @pl.when(qt == 0)
    def _():
        acc_ref[...] = diagonal_blocks()[1]

    @pl.when(qt > 0)
    def _():
        carry, acc = below_diagonal(0, *diagonal_blocks())
        carry_ref[...] = carry
        acc_ref[...] = acc

        def step(state):
            carry, acc = below_diagonal(state[0], carry_ref[...], acc_ref[...])
            carry_ref[...] = carry
            acc_ref[...] = acc
            return state[0] + 1, jnp.min(carry)

        n_iters = qt * (SB_QBLKS // SB_KEY_UNROLL)
        lax.while_loop(lambda s: jnp.logical_and(s[0] < n_iters, s[1] < SB_UNDERFLOW), step,
                       (jnp.int32(1), jnp.min(carry)))

    o_ref[...] = acc_ref[...].astype(o_ref.dtype)


def _sb_tri_weights():
    j = jnp.arange(2 * BLK)[:, None] % BLK
    c = jnp.arange(2 * BLK)[None, :]
    return ((c >= BLK) | (j >= c)).astype(BF16)


def _sb_attn(proj, batch, s_len):
    n_pairs = SB_WIDTH // LANES
    n_qt = s_len // SB_ROWS
    kv_blk = (s_len, LANES)
    return pl.pallas_call(
        _sb_kernel,
        grid=(batch, n_pairs, n_qt),
        in_specs=[
            pl.BlockSpec((SB_ROWS, LANES), lambda b, p, t: (b * n_qt + t, p)),
            pl.BlockSpec(kv_blk, lambda b, p, t: (b, n_pairs + p)),
            pl.BlockSpec(kv_blk, lambda b, p, t: (b, 2 * n_pairs + p)),
            _resident((2 * BLK, 2 * BLK)),
        ],
        out_specs=pl.BlockSpec((SB_ROWS, LANES), lambda b, p, t: (b * n_qt + t, p)),
        out_shape=jax.ShapeDtypeStruct((batch * s_len, SB_WIDTH), BF16),
        scratch_shapes=[pltpu.VMEM((SB_ROWS, 2 * BLK), F32), pltpu.VMEM((SB_ROWS, LANES), F32)],
        compiler_params=_params(("parallel", "parallel", "arbitrary")),
        name="sb_attn",
    )(proj, proj, proj, _sb_tri_weights())


MEM_ROWS = 512


def _mem_heads(q_ref, kv_ref, o_ref):
    s_len = q_ref.shape[0]
    mem_len = kv_ref.shape[0]
    head0 = _head0_lanes()
    n_pairs = MEM_WIDTH // LANES
    ksts = [_split_heads(kv_ref[:, p * LANES:(p + 1) * LANES], head0) for p in range(n_pairs)]
    vsts = [_with_den_cols(_split_heads(kv_ref[:, MEM_WIDTH + p * LANES:MEM_WIDTH + (p + 1) * LANES], head0))
            for p in range(n_pairs)]

    def step(c, carry):
        rows = pl.ds(pl.multiple_of(c * MEM_ROWS, MEM_ROWS), MEM_ROWS)
        for p in range(n_pairs):
            q = q_ref[rows, p * LANES:(p + 1) * LANES] * QK_SCALE
            z = lax.dot_general(q, ksts[p], NT_DIMS, preferred_element_type=F32)
            es = []
            for h in range(2):
                zh = z[:, h * mem_len:(h + 1) * mem_len]
                es.append(jnp.exp(zh - jnp.max(zh, axis=-1, keepdims=True)))
            od = jnp.dot(jnp.concatenate(es, axis=1).astype(BF16), vsts[p], preferred_element_type=F32)
            o_ref[rows, p * LANES:(p + 1) * LANES] = (od[:, :LANES] / od[:, LANES:]).astype(o_ref.dtype)
        return carry

    lax.fori_loop(0, s_len // MEM_ROWS, step, 0)


def _mem_attn(proj, q_col_block, kv_mem, batch, s_len):
    mem_len = kv_mem.shape[0] // batch
    return pl.pallas_call(
        _mem_heads,
        grid=(batch,),
        in_specs=[
            pl.BlockSpec((s_len, MEM_WIDTH), lambda b: (b, q_col_block)),
            pl.BlockSpec((mem_len, 2 * MEM_WIDTH), lambda b: (b, 0)),
        ],
        out_specs=pl.BlockSpec((s_len, MEM_WIDTH), lambda b: (b, 0)),
        out_shape=jax.ShapeDtypeStruct((batch * s_len, MEM_WIDTH), BF16),
        compiler_params=_params(("parallel",)),
        name="mem_attn",
    )(proj, kv_mem)


def _proj_b_kernel(x_ref, gkv_ref, gq_ref, wq_ref, wkv_ref, o0, o1, o2, stage_ref):
    tm = x_ref.shape[0]
    s_len = o0.shape[0]
    c = pl.program_id(1)
    gw = GROUP_WIDTH
    xhat = _rms_scale(x_ref[...])
    q = jnp.dot((xhat * gq_ref[...]).astype(BF16), wq_ref[...], preferred_element_type=F32)
    kv = jnp.dot((xhat * gkv_ref[...]).astype(BF16), wkv_ref[...], preferred_element_type=F32)
    for g, ((_, d), o_ref) in enumerate(zip(DIL_GROUPS, (o0, o1, o2))):
        qkv = jnp.concatenate([q[:, g * gw:(g + 1) * gw], kv[:, g * gw:(g + 1) * gw],
                               kv[:, DIL_WIDTH + g * gw:DIL_WIDTH + (g + 1) * gw]], axis=1)
        if d == 1:
            rows = pl.ds(pl.multiple_of(c * tm, tm), tm)
            o_ref[rows, :3 * gw] = qkv.astype(BF16)
            o_ref[rows, 3 * gw:] = q[:, DIL_WIDTH:].astype(BF16)
        else:
            n_cols = 3 * gw // LANES
            for j in range(n_cols):
                stage_ref[j] = qkv[:, j * LANES:(j + 1) * LANES]
            n = tm // d
            for r in range(d):
                dst = pl.ds(pl.multiple_of(r * (s_len // d) + c * n, n), n)
                o_ref[dst, :] = jnp.concatenate(
                    [stage_ref[j, pl.ds(r, n, stride=d), :] for j in range(n_cols)], axis=1).astype(BF16)


def _proj_b(h, g_kv, g_q, w_kv, w_in, batch, s_len):
    d = h.shape[1]
    tm = PROJ_ROWS
    n_tiles = s_len // tm
    widths = [3 * GROUP_WIDTH + MEM_WIDTH, 3 * GROUP_WIDTH, 3 * GROUP_WIDTH]
    return pl.pallas_call(
        _proj_b_kernel,
        grid=(batch, n_tiles),
        in_specs=[pl.BlockSpec((tm, d), lambda b, c: (b * n_tiles + c, 0)), _resident((1, d)), _resident((1, d)),
                  _resident(w_in.shape), _resident(w_kv.shape)],
        out_specs=[pl.BlockSpec((s_len, n), lambda b, c: (b, 0)) for n in widths],
        out_shape=[jax.ShapeDtypeStruct((batch * s_len, n), BF16) for n in widths],
        scratch_shapes=[pltpu.VMEM((3 * GROUP_WIDTH // LANES, tm, LANES), F32)],
        compiler_params=_params(("parallel", "arbitrary")),
        name="proj_b",
    )(h, g_kv.reshape(1, d), g_q.reshape(1, d), w_in.astype(BF16), w_kv.astype(BF16))


COMBINE_ROWS = 256


DIL_BLOCKS_PER_ITER = 4


def _softmax_parts(parts):
    m = parts[0]
    for t in parts[1:]:
        m = jnp.maximum(m, t)
    m = jnp.max(m, axis=-1, keepdims=True)
    return [jnp.exp(t - m) for t in parts], m


def _cross_attn_kernel(qkv0, qkv1, qkv2, kvm_ref, bias_first_ref, bias_ref, y0, y1, y2, om_ref,
                       o_cls, lse_cls, stage):
    s_len = qkv0.shape[0]
    head0 = _head0_lanes()
    gw = GROUP_WIDTH

    def block(g, qkv, p, row0, first):
        qc = slice(p * LANES, (p + 1) * LANES)
        kc = slice(gw + p * LANES, gw + (p + 1) * LANES)
        vc = slice(2 * gw + p * LANES, 2 * gw + (p + 1) * LANES)
        cur = pl.ds(pl.multiple_of(row0, BLK), BLK)
        q = qkv[cur, qc] * QK_SCALE
        kst = _split_heads(qkv[cur, kc], head0)
        vst = _with_den_cols(_split_heads(qkv[cur, vc], head0))
        if first:
            z = lax.dot_general(q, kst, NT_DIMS, preferred_element_type=F32) + bias_first_ref[g, p]
            heads = [[z[:, h * BLK:(h + 1) * BLK]] for h in range(2)]
        else:
            prev = pl.ds(pl.multiple_of(row0 - BLK, BLK), BLK)
            kst = jnp.concatenate([_split_heads(qkv[prev, kc], head0), kst], axis=0)
            vst = jnp.concatenate([_with_den_cols(_split_heads(qkv[prev, vc], head0)), vst], axis=0)
            z = lax.dot_general(q, kst, NT_DIMS, preferred_element_type=F32) + bias_ref[g, p]
            heads = [[z[:, h * BLK:(h + 1) * BLK], z[:, (2 + h) * BLK:(3 + h) * BLK]] for h in range(2)]
        (e0, m0), (e1, m1) = [_softmax_parts(parts) for parts in heads]
        pmat = jnp.concatenate([x for pair in zip(e0, e1) for x in pair], axis=1).astype(BF16)
        od = jnp.dot(pmat, vst, preferred_element_type=F32)
        o_cls[g, cur, qc] = od[:, :LANES] / od[:, LANES:]
        lse_cls[g, cur, qc] = jnp.where(head0, m0, m1) + jnp.log(od[:, LANES:])

    for g, ((_, d), qkv) in enumerate(zip(DIL_GROUPS, (qkv0, qkv1, qkv2))):
        blocks_per_class = s_len // d // BLK
        n_iters = s_len // BLK // DIL_BLOCKS_PER_ITER

        def blocks(it, starts_class, g=g, qkv=qkv, blocks_per_class=blocks_per_class):
            for k in range(DIL_BLOCKS_PER_ITER):
                first = blocks_per_class == 1 or (k == 0 and starts_class)
                for p in range(gw // LANES):
                    block(g, qkv, p, (it * DIL_BLOCKS_PER_ITER + k) * BLK, first)

        if blocks_per_class > DIL_BLOCKS_PER_ITER:
            blocks(0, True)
            lax.fori_loop(1, n_iters, lambda it, x, blocks=blocks: (blocks(it, False), x)[1], 0)
        else:
            lax.fori_loop(0, n_iters, lambda it, x, blocks=blocks: (blocks(it, True), x)[1], 0)

    def natural_rows(src, g, c, k):
        d = DIL_GROUPS[g][1]
        if d == 1:
            return src[g, pl.ds(pl.multiple_of(c * COMBINE_ROWS, COMBINE_ROWS), COMBINE_ROWS), :]
        n = COMBINE_ROWS // d
        n_cols = gw // LANES
        for r in range(d):
            rows = pl.ds(pl.multiple_of(r * (s_len // d) + c * n, n), n)
            for j in range(n_cols):
                stage[k * n_cols + j, pl.ds(r, n, stride=d), :] = src[g, rows, j * LANES:(j + 1) * LANES]
        return jnp.concatenate([stage[k * n_cols + j] for j in range(n_cols)], axis=1)

    def combine(c, carry):
        ls = [natural_rows(lse_cls, g, c, g) for g in range(3)]
        m = jnp.maximum(jnp.maximum(ls[0], ls[1]), ls[2])
        es = [jnp.exp(l - m) for l in ls]
        tot = es[0] + es[1] + es[2]
        rows = pl.ds(pl.multiple_of(c * COMBINE_ROWS, COMBINE_ROWS), COMBINE_ROWS)
        for g, y_ref in enumerate((y0, y1, y2)):
            y_ref[rows, :] = (natural_rows(o_cls, g, c, 3 + g) * (es[g] / tot)).astype(y_ref.dtype)
        return carry

    lax.fori_loop(0, s_len // COMBINE_ROWS, combine, 0)
    _mem_heads(qkv0.at[:, 3 * gw:], kvm_ref, om_ref)


def _dil_biases(slopes):
    i = jnp.arange(BLK)[:, None]
    j = jnp.arange(2 * BLK)[None, :]
    delta = i + BLK - j
    valid = (delta >= 0) & (delta <= BLK)
    firsts, others = [], []
    for g, (_, d) in enumerate(DIL_GROUPS):
        dist = (delta * d).astype(F32)
        f_p, o_p = [], []
        for p in range(2):
            halves = [jnp.where(valid, -slopes[g * HEADS_PER_GROUP + 2 * p + h] * dist, -jnp.inf) for h in range(2)]
            f_p.append(jnp.concatenate([halves[0][:, BLK:], halves[1][:, BLK:]], axis=1))
            o_p.append(jnp.concatenate([halves[0][:, :BLK], halves[1][:, :BLK],
                                        halves[0][:, BLK:], halves[1][:, BLK:]], axis=1))
        firsts.append(jnp.stack(f_p))
        others.append(jnp.stack(o_p))
    return jnp.stack(firsts), jnp.stack(others)


def _cross_attn(qkvs, kv_mem, batch, s_len):
    mem_len = kv_mem.shape[0] // batch
    slopes = 2.0 ** (-ALIBI_MAX_BIAS * jnp.arange(1, N_DIL_HEADS + 1, dtype=F32) / N_DIL_HEADS)
    bias_first, bias = _dil_biases(slopes)
    seq = lambda n: pl.BlockSpec((s_len, n), lambda b: (b, 0))
    n_out = len(DIL_GROUPS) + 1
    return pl.pallas_call(
        _cross_attn_kernel,
        grid=(batch,),
        in_specs=[seq(a.shape[1]) for a in qkvs]
        + [pl.BlockSpec((mem_len, 2 * MEM_WIDTH), lambda b: (b, 0)), _resident(bias_first.shape), _resident(bias.shape)],
        out_specs=[seq(GROUP_WIDTH)] * n_out,
        out_shape=[jax.ShapeDtypeStruct((batch * s_len, GROUP_WIDTH), BF16)] * n_out,
        scratch_shapes=[pltpu.VMEM((len(DIL_GROUPS), s_len, GROUP_WIDTH), F32)] * 2
        + [pltpu.VMEM((2 * len(DIL_GROUPS) * GROUP_WIDTH // LANES, COMBINE_ROWS, LANES), F32)],
        compiler_params=_params(("parallel",)),
        name="cross_attn",
    )(*qkvs, kv_mem, bias_first, bias)


FFN_ROWS = 512
FFN_CHUNK = 256
HALO = BF16_ROWS


def _out_ffn_kernel(n_attn, tiles_per_seq, final, h_ref, *refs):
    a_refs, wo_refs = refs[:n_attn], refs[n_attn:2 * n_attn]
    g_ref, wup_ref, wconv_ref, wdown_ref = refs[2 * n_attn:2 * n_attn + 4]
    rest = refs[2 * n_attn + 4:]
    if final:
        gf_ref, out_ref, xe_ref, act_ref = rest
    else:
        out_ref, xe_ref, act_ref = rest
    tm = h_ref.shape[0]
    i = pl.program_id(0)

    h = h_ref[...]
    for a_ref, wo_ref in zip(a_refs, wo_refs):
        h = h + jnp.dot(a_ref[...], wo_ref[...], preferred_element_type=F32)
    xn = _rms_scale(h) * g_ref[...]

    @pl.when(i % tiles_per_seq == 0)
    def _():
        xe_ref[:HALO, :] = jnp.zeros((HALO, xe_ref.shape[1]), xe_ref.dtype)

    @pl.when(i % tiles_per_seq != 0)
    def _():
        xe_ref[:HALO, :] = xe_ref[tm:, :]

    xe_ref[HALO:, :] = xn.astype(xe_ref.dtype)

    def conv(u, wc):
        return (wc[0:1] * pltpu.roll(u, 2, axis=0)[HALO:] + wc[1:2] * pltpu.roll(u, 1, axis=0)[HALO:]
                + wc[2:3] * u[HALO:])

    xe = xe_ref[...]
    for c in range(D_FF // FFN_CHUNK):
        ca_cols = slice(c * FFN_CHUNK, (c + 1) * FFN_CHUNK)
        cg_cols = slice(D_FF + c * FFN_CHUNK, D_FF + (c + 1) * FFN_CHUNK)
        ca = conv(jnp.dot(xe, wup_ref[:, ca_cols], preferred_element_type=F32), wconv_ref[:, ca_cols])
        cg = conv(jnp.dot(xe, wup_ref[:, cg_cols], preferred_element_type=F32), wconv_ref[:, cg_cols])
        half = 0.5 * cg
        act_ref[:, ca_cols] = ((half + half * jnp.tanh(half)) * ca).astype(act_ref.dtype)

    y = h + jnp.dot(act_ref[...], wdown_ref[...], preferred_element_type=F32)
    if final:
        y = _rms_scale(y) * gf_ref[...]
    out_ref[...] = y


def _out_ffn(h, attn_parts, w_out, g_ffn, w_up, w_conv, w_down, s_len, g_final=None):
    t, d = h.shape
    tm = FFN_ROWS
    n_attn = len(attn_parts)
    final = g_final is not None
    offs = [0]
    for a in attn_parts:
        offs.append(offs[-1] + a.shape[1])
    wo_parts = [w_out[offs[k]:offs[k + 1]].astype(BF16) for k in range(n_attn)]
    row = lambda n: pl.BlockSpec((tm, n), lambda i: (i, 0))
    in_specs = [row(d)] + [row(a.shape[1]) for a in attn_parts] + [_resident(w.shape) for w in wo_parts]
    in_specs += [_resident((1, d)), _resident(w_up.shape), _resident(w_conv.shape), _resident(w_down.shape)]
    args = [h, *attn_parts, *wo_parts, g_ffn.reshape(1, d), w_up.astype(BF16), w_conv, w_down.astype(BF16)]
    if final:
        in_specs.append(_resident((1, d)))
        args.append(g_final.reshape(1, d))
    return pl.pallas_call(
        functools.partial(_out_ffn_kernel, n_attn, s_len // tm, final),
        grid=(t // tm,),
        in_specs=in_specs,
        out_specs=row(d),
        out_shape=jax.ShapeDtypeStruct((t, d), F32),
        scratch_shapes=[pltpu.VMEM((HALO + tm, d), BF16), pltpu.VMEM((tm, D_FF), BF16)],
        compiler_params=_params(("arbitrary",)),
        name="out_ffn",
    )(*args)


def kernel(x, mem, a_norm_attn, a_w_in, a_w_out, a_norm_mem, a_w_mem_kv, a_norm_ffn, a_ffn_up, a_ffn_conv, a_ffn_down, kv_norm, w_kv_shared, b_norm_attn, b_w_in, b_w_out, b_norm_mem, b_w_mem_kv, b_norm_ffn, b_ffn_up, b_ffn_conv, b_ffn_down, final_norm):
    batch, s_len, d = x.shape
    assert a_w_in.shape[0] == 1 and b_w_in.shape[0] == 1, "one self-decoder and one cross-decoder layer"
    assert d == D_MODEL and s_len % (BLK * DIL_GROUPS[-1][1]) == 0 and s_len % FFN_ROWS == 0
    t = batch * s_len
    h = x.reshape(t, d)
    mem2 = mem.reshape(batch * mem.shape[1], d)

    kvm_a, kvm_b = _rms_proj(mem2, [a_norm_mem[0], b_norm_mem[0]],
                             [a_w_mem_kv[0].astype(BF16), b_w_mem_kv[0].astype(BF16)], PROJ_ROWS)

    (proj_a,) = _rms_proj(h, [a_norm_attn[0]], [a_w_in[0].astype(BF16)], PROJ_ROWS)
    o_sb = _sb_attn(proj_a, batch, s_len)
    o_mem = _mem_attn(proj_a, 3 * SB_WIDTH // MEM_WIDTH, kvm_a, batch, s_len)
    h = _out_ffn(h, [o_sb, o_mem], a_w_out[0], a_norm_ffn[0], a_ffn_up[0], a_ffn_conv[0], a_ffn_down[0], s_len)

    qkvs = _proj_b(h, kv_norm, b_norm_attn[0], w_kv_shared, b_w_in[0], batch, s_len)
    attn_parts = _cross_attn(qkvs, kvm_b, batch, s_len)
    h = _out_ffn(h, attn_parts, b_w_out[0], b_norm_ffn[0], b_ffn_up[0], b_ffn_conv[0], b_ffn_down[0], s_len,
                 g_final=final_norm)
    return h.reshape(batch, s_len, d)
```

```python
import functools

import jax
import jax.numpy as jnp
from jax import lax
from jax.experimental import pallas as pl
from jax.experimental.pallas import tpu as pltpu

D_MODEL = 1024
HEAD_DIM = 64
N_SB_HEADS = 12
N_MEM_HEADS = 4
DIL_GROUPS = ((128, 1), (512, 4), (2048, 16))
HEADS_PER_GROUP = 4
N_DIL_HEADS = HEADS_PER_GROUP * len(DIL_GROUPS)
SB_WIDTH = N_SB_HEADS * HEAD_DIM
MEM_WIDTH = N_MEM_HEADS * HEAD_DIM
DIL_WIDTH = N_DIL_HEADS * HEAD_DIM
GROUP_WIDTH = HEADS_PER_GROUP * HEAD_DIM
D_FF = 2816
CONV_WIDTH = 3
EPS = 1e-6
ALIBI_MAX_BIAS = 8.0
QK_SCALE = HEAD_DIM ** -0.5
LOG2E = 1.4426950408889634

LANES = 128
BF16_ROWS = 16
VMEM_LIMIT_BYTES = 56 * 1024 * 1024

BLK = 128
F32 = jnp.float32
BF16 = jnp.bfloat16
NT_DIMS = (((1,), (1,)), ((), ()))


def _params(semantics):
    return pltpu.CompilerParams(dimension_semantics=semantics, vmem_limit_bytes=VMEM_LIMIT_BYTES)


def _resident(shape):
    return pl.BlockSpec(shape, lambda *_: (0,) * len(shape), pipeline_mode=pl.Buffered(1))


def _head0_lanes():
    return lax.broadcasted_iota(jnp.int32, (1, LANES), 1) < HEAD_DIM


def _split_heads(t, head0):
    zero = jnp.zeros_like(t)
    return jnp.concatenate([jnp.where(head0, t, zero), jnp.where(head0, zero, t)], axis=0)


def _with_den_cols(vst):
    row = lax.broadcasted_iota(jnp.int32, vst.shape, 0)
    lane = lax.broadcasted_iota(jnp.int32, vst.shape, 1)
    owns = (row < vst.shape[0] // 2) == (lane < HEAD_DIM)
    return jnp.concatenate([vst, jnp.where(owns, 1.0, 0.0).astype(vst.dtype)], axis=1)


def _rms_scale(x):
    return x * lax.rsqrt(jnp.mean(x * x, axis=-1, keepdims=True) + EPS)


PROJ_ROWS = 512


def _rms_proj_kernel(n_out, x_ref, *refs):
    g_refs, w_refs, o_refs = refs[:n_out], refs[n_out:2 * n_out], refs[2 * n_out:]
    xhat = _rms_scale(x_ref[...])
    for g_ref, w_ref, o_ref in zip(g_refs, w_refs, o_refs):
        xn = (xhat * g_ref[...]).astype(BF16)
        o_ref[...] = jnp.dot(xn, w_ref[...], preferred_element_type=F32).astype(o_ref.dtype)


def _rms_proj(x, gains, weights, tm):
    t, d = x.shape
    n_out = len(gains)
    in_specs = [pl.BlockSpec((tm, d), lambda i: (i, 0))]
    in_specs += [_resident((1, d)) for _ in gains]
    in_specs += [_resident(w.shape) for w in weights]
    out_specs = [pl.BlockSpec((tm, w.shape[1]), lambda i: (i, 0)) for w in weights]
    out_shape = [jax.ShapeDtypeStruct((t, w.shape[1]), BF16) for w in weights]
    return pl.pallas_call(
        functools.partial(_rms_proj_kernel, n_out),
        grid=(t // tm,),
        in_specs=in_specs,
        out_specs=out_specs,
        out_shape=out_shape,
        compiler_params=_params(("parallel",)),
        name="rms_proj",
    )(x, *[g.reshape(1, d) for g in gains], *weights)


SB_QBLKS = 4
SB_ROWS = SB_QBLKS * BLK
SB_PAIRS = 2
SB_BAND = 3
SB_UNDERFLOW = 106.0


def _sb_kernel(q_ref, k_ref, v_ref, tri_ref, o_ref, carry_ref, acc_ref):
    qt = pl.program_id(2)
    head0 = _head0_lanes()
    tri = tri_ref[...]
    pairs = range(SB_PAIRS)
    q_all = [q_ref[:, p * LANES:(p + 1) * LANES] * QK_SCALE for p in pairs]

    def key_block(p, k0, row_lo, row_hi, diagonal, carry, acc, live_from=None):
        cols = slice(p * LANES, (p + 1) * LANES)
        kst = _split_heads(k_ref[pl.ds(k0, BLK), cols], head0)
        vst = _split_heads(v_ref[pl.ds(k0, BLK), cols], head0)
        z = lax.dot_general(q_all[p][row_lo:row_hi], kst, NT_DIMS, preferred_element_type=F32)
        sp = jnp.maximum(z, 0.0) + jnp.log(1.0 + jnp.exp2(jnp.abs(z) * (-LOG2E)))
        if diagonal:
            t_rel = lax.broadcasted_iota(jnp.int32, (BLK, 2 * BLK), 0)
            s_rel = lax.broadcasted_iota(jnp.int32, (BLK, 2 * BLK), 1) & (BLK - 1)
            causal = s_rel < t_rel

            def mask(t):
                top = jnp.where(causal, t[:BLK], 0.0)
                return top if t.shape[0] == BLK else jnp.concatenate([top, t[BLK:]], axis=0)
        elif live_from is not None:
            live = lax.broadcasted_iota(jnp.int32, z.shape, 0) + row_lo >= live_from

            def mask(t):
                return jnp.where(live, t, 0.0)
        else:
            def mask(t):
                return t

        sp = mask(sp)
        hi = sp.astype(BF16)
        lo = (sp - hi.astype(F32)).astype(BF16)
        sums = [jnp.dot(jnp.concatenate([hi[:, h * BLK:(h + 1) * BLK], lo[:, h * BLK:(h + 1) * BLK]], axis=1),
                        tri, preferred_element_type=F32) for h in range(2)]
        suffix = jnp.concatenate([s[:, :BLK] for s in sums], axis=1)
        total = jnp.concatenate([s[:, BLK:] for s in sums], axis=1)
        w = mask(jnp.exp2((z - (suffix + carry[row_lo:row_hi])) * LOG2E))
        pv = jnp.dot(w.astype(BF16), vst, preferred_element_type=F32)

        def all_rows(t):
            parts = [jnp.zeros((n, t.shape[1]), F32) if n else None for n in (row_lo, SB_ROWS - row_hi)]
            parts = [x for x in (parts[0], t, parts[1]) if x is not None]
            return t if len(parts) == 1 else jnp.concatenate(parts, axis=0)

        return carry + all_rows(total), acc + all_rows(pv)

    r0 = qt * SB_ROWS

    def band(lowest):
        carry = [jnp.zeros((SB_ROWS, 2 * BLK), F32) for _ in pairs]
        acc = [jnp.zeros((SB_ROWS, LANES), F32) for _ in pairs]
        for j in range(SB_QBLKS - 1, lowest - 1, -1):
            for p in pairs:
                carry[p], acc[p] = key_block(p, pl.multiple_of(r0 + j * BLK, BLK), max(j, 0) * BLK,
                                             min(j + SB_BAND, SB_QBLKS) * BLK, j >= 0, carry[p], acc[p])
        for p in pairs:
            carry_ref[p] = carry[p]
            acc_ref[p] = acc[p]

    @pl.when(qt == 0)
    def _():
        band(0)

    @pl.when(qt > 0)
    def _():
        band(1 - SB_BAND)

    top = SB_QBLKS - 1 - SB_BAND
    n_steps = top + 1 + qt * SB_QBLKS

    def step(state):
        j = top - state[0]
        lowest = None
        for p in pairs:
            carry, acc = key_block(p, pl.multiple_of(r0 + j * BLK, BLK), 0, SB_ROWS, False, carry_ref[p], acc_ref[p],
                                   live_from=(j + SB_BAND) * BLK)
            carry_ref[p] = carry
            acc_ref[p] = acc
            lowest = jnp.min(carry) if lowest is None else jnp.minimum(lowest, jnp.min(carry))
        return state[0] + 1, lowest

    lax.while_loop(lambda s: jnp.logical_and(s[0] < n_steps, s[1] < SB_UNDERFLOW), step,
                   (jnp.int32(0), jnp.min(carry_ref[...])))
    for p in pairs:
        o_ref[:, p * LANES:(p + 1) * LANES] = acc_ref[p].astype(o_ref.dtype)


def _sb_tri_weights():
    j = jnp.arange(2 * BLK)[:, None] % BLK
    c = jnp.arange(2 * BLK)[None, :]
    return ((c >= BLK) | (j >= c)).astype(BF16)


def _sb_attn(proj, batch, s_len):
    width = SB_PAIRS * LANES
    n_groups = SB_WIDTH // width
    n_qt = s_len // SB_ROWS
    kv_blk = (s_len, width)
    return pl.pallas_call(
        _sb_kernel,
        grid=(batch, n_groups, n_qt),
        in_specs=[
            pl.BlockSpec((SB_ROWS, width), lambda b, p, t: (b * n_qt + t, p)),
            pl.BlockSpec(kv_blk, lambda b, p, t: (b, n_groups + p)),
            pl.BlockSpec(kv_blk, lambda b, p, t: (b, 2 * n_groups + p)),
            _resident((2 * BLK, 2 * BLK)),
        ],
        out_specs=pl.BlockSpec((SB_ROWS, width), lambda b, p, t: (b * n_qt + t, p)),
        out_shape=jax.ShapeDtypeStruct((batch * s_len, SB_WIDTH), BF16),
        scratch_shapes=[pltpu.VMEM((SB_PAIRS, SB_ROWS, 2 * BLK), F32), pltpu.VMEM((SB_PAIRS, SB_ROWS, LANES), F32)],
        compiler_params=_params(("parallel", "parallel", "arbitrary")),
        name="sb_attn",
    )(proj, proj, proj, _sb_tri_weights())


MEM_ROWS = 512


def _mem_heads(q_ref, kv_ref, o_ref):
    s_len = q_ref.shape[0]
    mem_len = kv_ref.shape[0]
    head0 = _head0_lanes()
    n_pairs = MEM_WIDTH // LANES
    ksts = [_split_heads(kv_ref[:, p * LANES:(p + 1) * LANES], head0) for p in range(n_pairs)]
    vsts = [_with_den_cols(_split_heads(kv_ref[:, MEM_WIDTH + p * LANES:MEM_WIDTH + (p + 1) * LANES], head0))
            for p in range(n_pairs)]

    def step(c, carry):
        rows = pl.ds(pl.multiple_of(c * MEM_ROWS, MEM_ROWS), MEM_ROWS)
        for p in range(n_pairs):
            q = q_ref[rows, p * LANES:(p + 1) * LANES] * QK_SCALE
            z = lax.dot_general(q, ksts[p], NT_DIMS, preferred_element_type=F32)
            es = []
            for h in range(2):
                zh = z[:, h * mem_len:(h + 1) * mem_len]
                es.append(jnp.exp(zh - jnp.max(zh, axis=-1, keepdims=True)))
            od = jnp.dot(jnp.concatenate(es, axis=1).astype(BF16), vsts[p], preferred_element_type=F32)
            o_ref[rows, p * LANES:(p + 1) * LANES] = (od[:, :LANES] / od[:, LANES:]).astype(o_ref.dtype)
        return carry

    lax.fori_loop(0, s_len // MEM_ROWS, step, 0)


def _mem_attn(proj, q_col_block, kv_mem, batch, s_len):
    mem_len = kv_mem.shape[0] // batch
    return pl.pallas_call(
        _mem_heads,
        grid=(batch,),
        in_specs=[
            pl.BlockSpec((s_len, MEM_WIDTH), lambda b: (b, q_col_block)),
            pl.BlockSpec((mem_len, 2 * MEM_WIDTH), lambda b: (b, 0)),
        ],
        out_specs=pl.BlockSpec((s_len, MEM_WIDTH), lambda b: (b, 0)),
        out_shape=jax.ShapeDtypeStruct((batch * s_len, MEM_WIDTH), BF16),
        compiler_params=_params(("parallel",)),
        name="mem_attn",
    )(proj, kv_mem)


def _proj_b_kernel(x_ref, gkv_ref, gq_ref, wq_ref, wkv_ref, o0, o1, o2, stage_ref):
    tm = x_ref.shape[0]
    s_len = o0.shape[0]
    c = pl.program_id(1)
    gw = GROUP_WIDTH
    xhat = _rms_scale(x_ref[...])
    q = jnp.dot((xhat * gq_ref[...]).astype(BF16), wq_ref[...], preferred_element_type=F32)
    kv = jnp.dot((xhat * gkv_ref[...]).astype(BF16), wkv_ref[...], preferred_element_type=F32)
    for g, ((_, d), o_ref) in enumerate(zip(DIL_GROUPS, (o0, o1, o2))):
        qkv = jnp.concatenate([q[:, g * gw:(g + 1) * gw], kv[:, g * gw:(g + 1) * gw],
                               kv[:, DIL_WIDTH + g * gw:DIL_WIDTH + (g + 1) * gw]], axis=1)
        if d == 1:
            rows = pl.ds(pl.multiple_of(c * tm, tm), tm)
            o_ref[rows, :3 * gw] = qkv.astype(BF16)
            o_ref[rows, 3 * gw:] = q[:, DIL_WIDTH:].astype(BF16)
        else:
            n_cols = 3 * gw // LANES
            for j in range(n_cols):
                stage_ref[j] = qkv[:, j * LANES:(j + 1) * LANES]
            n = tm // d
            for r in range(d):
                dst = pl.ds(pl.multiple_of(r * (s_len // d) + c * n, n), n)
                o_ref[dst, :] = jnp.concatenate(
                    [stage_ref[j, pl.ds(r, n, stride=d), :] for j in range(n_cols)], axis=1).astype(BF16)


def _proj_b(h, g_kv, g_q, w_kv, w_in, batch, s_len):
    d = h.shape[1]
    tm = PROJ_ROWS
    n_tiles = s_len // tm
    widths = [3 * GROUP_WIDTH + MEM_WIDTH, 3 * GROUP_WIDTH, 3 * GROUP_WIDTH]
    return pl.pallas_call(
        _proj_b_kernel,
        grid=(batch, n_tiles),
        in_specs=[pl.BlockSpec((tm, d), lambda b, c: (b * n_tiles + c, 0)), _resident((1, d)), _resident((1, d)),
                  _resident(w_in.shape), _resident(w_kv.shape)],
        out_specs=[pl.BlockSpec((s_len, n), lambda b, c: (b, 0)) for n in widths],
        out_shape=[jax.ShapeDtypeStruct((batch * s_len, n), BF16) for n in widths],
        scratch_shapes=[pltpu.VMEM((3 * GROUP_WIDTH // LANES, tm, LANES), F32)],
        compiler_params=_params(("parallel", "arbitrary")),
        name="proj_b",
    )(h, g_kv.reshape(1, d), g_q.reshape(1, d), w_in.astype(BF16), w_kv.astype(BF16))


COMBINE_ROWS = 256


DIL_BLOCKS_PER_ITER = 4


def _softmax_parts(parts):
    m = parts[0]
    for t in parts[1:]:
        m = jnp.maximum(m, t)
    m = jnp.max(m, axis=-1, keepdims=True)
    return [jnp.exp(t - m) for t in parts], m


def _cross_attn_kernel(qkv0, qkv1, qkv2, kvm_ref, bias_first_ref, bias_ref, y0, y1, y2, om_ref,
                       o_cls, lse_cls, stage):
    s_len = qkv0.shape[0]
    head0 = _head0_lanes()
    gw = GROUP_WIDTH

    def block(g, qkv, p, row0, first):
        qc = slice(p * LANES, (p + 1) * LANES)
        kc = slice(gw + p * LANES, gw + (p + 1) * LANES)
        vc = slice(2 * gw + p * LANES, 2 * gw + (p + 1) * LANES)
        cur = pl.ds(pl.multiple_of(row0, BLK), BLK)
        q = qkv[cur, qc] * QK_SCALE
        kst = _split_heads(qkv[cur, kc], head0)
        vst = _with_den_cols(_split_heads(qkv[cur, vc], head0))
        if first:
            z = lax.dot_general(q, kst, NT_DIMS, preferred_element_type=F32) + bias_first_ref[g, p]
            heads = [[z[:, h * BLK:(h + 1) * BLK]] for h in range(2)]
        else:
            prev = pl.ds(pl.multiple_of(row0 - BLK, BLK), BLK)
            kst = jnp.concatenate([_split_heads(qkv[prev, kc], head0), kst], axis=0)
            vst = jnp.concatenate([_with_den_cols(_split_heads(qkv[prev, vc], head0)), vst], axis=0)
            z = lax.dot_general(q, kst, NT_DIMS, preferred_element_type=F32) + bias_ref[g, p]
            heads = [[z[:, h * BLK:(h + 1) * BLK], z[:, (2 + h) * BLK:(3 + h) * BLK]] for h in range(2)]
        (e0, m0), (e1, m1) = [_softmax_parts(parts) for parts in heads]
        pmat = jnp.concatenate([x for pair in zip(e0, e1) for x in pair], axis=1).astype(BF16)
        od = jnp.dot(pmat, vst, preferred_element_type=F32)
        o_cls[g, cur, qc] = od[:, :LANES] / od[:, LANES:]
        lse_cls[g, cur, qc] = jnp.where(head0, m0, m1) + jnp.log(od[:, LANES:])

    for g, ((_, d), qkv) in enumerate(zip(DIL_GROUPS, (qkv0, qkv1, qkv2))):
        blocks_per_class = s_len // d // BLK
        n_iters = s_len // BLK // DIL_BLOCKS_PER_ITER

        def blocks(it, starts_class, g=g, qkv=qkv, blocks_per_class=blocks_per_class):
            for k in range(DIL_BLOCKS_PER_ITER):
                first = blocks_per_class == 1 or (k == 0 and starts_class)
                for p in range(gw // LANES):
                    block(g, qkv, p, (it * DIL_BLOCKS_PER_ITER + k) * BLK, first)

        if blocks_per_class > DIL_BLOCKS_PER_ITER:
            blocks(0, True)
            lax.fori_loop(1, n_iters, lambda it, x, blocks=blocks: (blocks(it, False), x)[1], 0)
        else:
            lax.fori_loop(0, n_iters, lambda it, x, blocks=blocks: (blocks(it, True), x)[1], 0)

    def natural_rows(src, g, c, k):
        d = DIL_GROUPS[g][1]
        if d == 1:
            return src[g, pl.ds(pl.multiple_of(c * COMBINE_ROWS, COMBINE_ROWS), COMBINE_ROWS), :]
        n = COMBINE_ROWS // d
        n_cols = gw // LANES
        for r in range(d):
            rows = pl.ds(pl.multiple_of(r * (s_len // d) + c * n, n), n)
            for j in range(n_cols):
                stage[k * n_cols + j, pl.ds(r, n, stride=d), :] = src[g, rows, j * LANES:(j + 1) * LANES]
        return jnp.concatenate([stage[k * n_cols + j] for j in range(n_cols)], axis=1)

    def combine(c, carry):
        ls = [natural_rows(lse_cls, g, c, g) for g in range(3)]
        m = jnp.maximum(jnp.maximum(ls[0], ls[1]), ls[2])
        es = [jnp.exp(l - m) for l in ls]
        tot = es[0] + es[1] + es[2]
        rows = pl.ds(pl.multiple_of(c * COMBINE_ROWS, COMBINE_ROWS), COMBINE_ROWS)
        for g, y_ref in enumerate((y0, y1, y2)):
            y_ref[rows, :] = (natural_rows(o_cls, g, c, 3 + g) * (es[g] / tot)).astype(y_ref.dtype)
        return carry

    lax.fori_loop(0, s_len // COMBINE_ROWS, combine, 0)
    _mem_heads(qkv0.at[:, 3 * gw:], kvm_ref, om_ref)


def _dil_biases(slopes):
    i = jnp.arange(BLK)[:, None]
    j = jnp.arange(2 * BLK)[None, :]
    delta = i + BLK - j
    valid = (delta >= 0) & (delta <= BLK)
    firsts, others = [], []
    for g, (_, d) in enumerate(DIL_GROUPS):
        dist = (delta * d).astype(F32)
        f_p, o_p = [], []
        for p in range(2):
            halves = [jnp.where(valid, -slopes[g * HEADS_PER_GROUP + 2 * p + h] * dist, -jnp.inf) for h in range(2)]
            f_p.append(jnp.concatenate([halves[0][:, BLK:], halves[1][:, BLK:]], axis=1))
            o_p.append(jnp.concatenate([halves[0][:, :BLK], halves[1][:, :BLK],
                                        halves[0][:, BLK:], halves[1][:, BLK:]], axis=1))
        firsts.append(jnp.stack(f_p))
        others.append(jnp.stack(o_p))
    return jnp.stack(firsts), jnp.stack(others)


def _cross_attn(qkvs, kv_mem, batch, s_len):
    mem_len = kv_mem.shape[0] // batch
    slopes = 2.0 ** (-ALIBI_MAX_BIAS * jnp.arange(1, N_DIL_HEADS + 1, dtype=F32) / N_DIL_HEADS)
    bias_first, bias = _dil_biases(slopes)
    seq = lambda n: pl.BlockSpec((s_len, n), lambda b: (b, 0))
    n_out = len(DIL_GROUPS) + 1
    return pl.pallas_call(
        _cross_attn_kernel,
        grid=(batch,),
        in_specs=[seq(a.shape[1]) for a in qkvs]
        + [pl.BlockSpec((mem_len, 2 * MEM_WIDTH), lambda b: (b, 0)), _resident(bias_first.shape), _resident(bias.shape)],
        out_specs=[seq(GROUP_WIDTH)] * n_out,
        out_shape=[jax.ShapeDtypeStruct((batch * s_len, GROUP_WIDTH), BF16)] * n_out,
        scratch_shapes=[pltpu.VMEM((len(DIL_GROUPS), s_len, GROUP_WIDTH), F32)] * 2
        + [pltpu.VMEM((2 * len(DIL_GROUPS) * GROUP_WIDTH // LANES, COMBINE_ROWS, LANES), F32)],
        compiler_params=_params(("parallel",)),
        name="cross_attn",
    )(*qkvs, kv_mem, bias_first, bias)


FFN_ROWS = 512
FFN_CHUNK = 256
HALO = BF16_ROWS


def _out_ffn_kernel(n_attn, tiles_per_seq, final, h_ref, *refs):
    a_refs, wo_refs = refs[:n_attn], refs[n_attn:2 * n_attn]
    g_ref, wup_ref, wconv_ref, wdown_ref = refs[2 * n_attn:2 * n_attn + 4]
    rest = refs[2 * n_attn + 4:]
    if final:
        gf_ref, out_ref, xe_ref, act_ref = rest
    else:
        out_ref, xe_ref, act_ref = rest
    tm = h_ref.shape[0]
    i = pl.program_id(0)

    h = h_ref[...]
    for a_ref, wo_ref in zip(a_refs, wo_refs):
        h = h + jnp.dot(a_ref[...], wo_ref[...], preferred_element_type=F32)
    xn = _rms_scale(h) * g_ref[...]

    @pl.when(i % tiles_per_seq == 0)
    def _():
        xe_ref[:HALO, :] = jnp.zeros((HALO, xe_ref.shape[1]), xe_ref.dtype)

    @pl.when(i % tiles_per_seq != 0)
    def _():
        xe_ref[:HALO, :] = xe_ref[tm:, :]

    xe_ref[HALO:, :] = xn.astype(xe_ref.dtype)

    def conv(u, wc):
        return (wc[0:1] * pltpu.roll(u, 2, axis=0)[HALO:] + wc[1:2] * pltpu.roll(u, 1, axis=0)[HALO:]
                + wc[2:3] * u[HALO:])

    xe = xe_ref[...]
    for c in range(D_FF // FFN_CHUNK):
        ca_cols = slice(c * FFN_CHUNK, (c + 1) * FFN_CHUNK)
        cg_cols = slice(D_FF + c * FFN_CHUNK, D_FF + (c + 1) * FFN_CHUNK)
        ca = conv(jnp.dot(xe, wup_ref[:, ca_cols], preferred_element_type=F32), wconv_ref[:, ca_cols])
        cg = conv(jnp.dot(xe, wup_ref[:, cg_cols], preferred_element_type=F32), wconv_ref[:, cg_cols])
        half = 0.5 * cg
        act_ref[:, ca_cols] = ((half + half * jnp.tanh(half)) * ca).astype(act_ref.dtype)

    y = h + jnp.dot(act_ref[...], wdown_ref[...], preferred_element_type=F32)
    if final:
        y = _rms_scale(y) * gf_ref[...]
    out_ref[...] = y


def _out_ffn(h, attn_parts, w_out, g_ffn, w_up, w_conv, w_down, s_len, g_final=None):
    t, d = h.shape
    tm = FFN_ROWS
    n_attn = len(attn_parts)
    final = g_final is not None
    offs = [0]
    for a in attn_parts:
        offs.append(offs[-1] + a.shape[1])
    wo_parts = [w_out[offs[k]:offs[k + 1]].astype(BF16) for k in range(n_attn)]
    row = lambda n: pl.BlockSpec((tm, n), lambda i: (i, 0))
    in_specs = [row(d)] + [row(a.shape[1]) for a in attn_parts] + [_resident(w.shape) for w in wo_parts]
    in_specs += [_resident((1, d)), _resident(w_up.shape), _resident(w_conv.shape), _resident(w_down.shape)]
    args = [h, *attn_parts, *wo_parts, g_ffn.reshape(1, d), w_up.astype(BF16), w_conv, w_down.astype(BF16)]
    if final:
        in_specs.append(_resident((1, d)))
        args.append(g_final.reshape(1, d))
    return pl.pallas_call(
        functools.partial(_out_ffn_kernel, n_attn, s_len // tm, final),
        grid=(t // tm,),
        in_specs=in_specs,
        out_specs=row(d),
        out_shape=jax.ShapeDtypeStruct((t, d), F32),
        scratch_shapes=[pltpu.VMEM((HALO + tm, d), BF16), pltpu.VMEM((tm, D_FF), BF16)],
        compiler_params=_params(("arbitrary",)),
        name="out_ffn",
    )(*args)


def kernel(x, mem, a_norm_attn, a_w_in, a_w_out, a_norm_mem, a_w_mem_kv, a_norm_ffn, a_ffn_up, a_ffn_conv, a_ffn_down, kv_norm, w_kv_shared, b_norm_attn, b_w_in, b_w_out, b_norm_mem, b_w_mem_kv, b_norm_ffn, b_ffn_up, b_ffn_conv, b_ffn_down, final_norm):
    batch, s_len, d = x.shape
    assert a_w_in.shape[0] == 1 and b_w_in.shape[0] == 1, "one self-decoder and one cross-decoder layer"
    assert d == D_MODEL and s_len % (BLK * DIL_GROUPS[-1][1]) == 0 and s_len % FFN_ROWS == 0
    t = batch * s_len
    h = x.reshape(t, d)
    mem2 = mem.reshape(batch * mem.shape[1], d)

    kvm_a, kvm_b = _rms_proj(mem2, [a_norm_mem[0], b_norm_mem[0]],
                             [a_w_mem_kv[0].astype(BF16), b_w_mem_kv[0].astype(BF16)], PROJ_ROWS)

    (proj_a,) = _rms_proj(h, [a_norm_attn[0]], [a_w_in[0].astype(BF16)], PROJ_ROWS)
    o_sb = _sb_attn(proj_a, batch, s_len)
    o_mem = _mem_attn(proj_a, 3 * SB_WIDTH // MEM_WIDTH, kvm_a, batch, s_len)
    h = _out_ffn(h, [o_sb, o_mem], a_w_out[0], a_norm_ffn[0], a_ffn_up[0], a_ffn_conv[0], a_ffn_down[0], s_len)

    qkvs = _proj_b(h, kv_norm, b_norm_attn[0], w_kv_shared, b_w_in[0], batch, s_len)
    attn_parts = _cross_attn(qkvs, kvm_b, batch, s_len)
    h = _out_ffn(h, attn_parts, b_w_out[0], b_norm_ffn[0], b_ffn_up[0], b_ffn_conv[0], b_ffn_down[0], s_len,
                 g_final=final_norm)
    return h.reshape(batch, s_len, d)
```

```python
import functools

import jax
import jax.numpy as jnp
from jax import lax
from jax.experimental import pallas as pl
from jax.experimental.pallas import tpu as pltpu

D_MODEL = 1024
HEAD_DIM = 64
N_SB_HEADS = 12
N_MEM_HEADS = 4
DIL_GROUPS = ((128, 1), (512, 4), (2048, 16))
HEADS_PER_GROUP = 4
N_DIL_HEADS = HEADS_PER_GROUP * len(DIL_GROUPS)
SB_WIDTH = N_SB_HEADS * HEAD_DIM
MEM_WIDTH = N_MEM_HEADS * HEAD_DIM
DIL_WIDTH = N_DIL_HEADS * HEAD_DIM
GROUP_WIDTH = HEADS_PER_GROUP * HEAD_DIM
D_FF = 2816
CONV_WIDTH = 3
EPS = 1e-6
ALIBI_MAX_BIAS = 8.0
QK_SCALE = HEAD_DIM ** -0.5
LOG2E = 1.4426950408889634

LANES = 128
BF16_ROWS = 16
VMEM_LIMIT_BYTES = 56 * 1024 * 1024

BLK = 128
F32 = jnp.float32
BF16 = jnp.bfloat16
NT_DIMS = (((1,), (1,)), ((), ()))


def _params(semantics):
    return pltpu.CompilerParams(dimension_semantics=semantics, vmem_limit_bytes=VMEM_LIMIT_BYTES)


def _resident(shape):
    return pl.BlockSpec(shape, lambda *_: (0,) * len(shape), pipeline_mode=pl.Buffered(1))


def _head0_lanes():
    return lax.broadcasted_iota(jnp.int32, (1, LANES), 1) < HEAD_DIM


def _split_heads(t, head0):
    zero = jnp.zeros_like(t)
    return jnp.concatenate([jnp.where(head0, t, zero), jnp.where(head0, zero, t)], axis=0)


def _with_den_cols(vst):
    row = lax.broadcasted_iota(jnp.int32, vst.shape, 0)
    lane = lax.broadcasted_iota(jnp.int32, vst.shape, 1)
    owns = (row < vst.shape[0] // 2) == (lane < HEAD_DIM)
    return jnp.concatenate([vst, jnp.where(owns, 1.0, 0.0).astype(vst.dtype)], axis=1)


def _rms_scale(x):
    return x * lax.rsqrt(jnp.mean(x * x, axis=-1, keepdims=True) + EPS)


PROJ_ROWS = 512


def _rms_proj_kernel(n_out, x_ref, *refs):
    g_refs, w_refs, o_refs = refs[:n_out], refs[n_out:2 * n_out], refs[2 * n_out:]
    xhat = _rms_scale(x_ref[...])
    for g_ref, w_ref, o_ref in zip(g_refs, w_refs, o_refs):
        xn = (xhat * g_ref[...]).astype(BF16)
        o_ref[...] = jnp.dot(xn, w_ref[...], preferred_element_type=F32).astype(o_ref.dtype)


def _rms_proj(x, gains, weights, tm):
    t, d = x.shape
    n_out = len(gains)
    in_specs = [pl.BlockSpec((tm, d), lambda i: (i, 0))]
    in_specs += [_resident((1, d)) for _ in gains]
    in_specs += [_resident(w.shape) for w in weights]
    out_specs = [pl.BlockSpec((tm, w.shape[1]), lambda i: (i, 0)) for w in weights]
    out_shape = [jax.ShapeDtypeStruct((t, w.shape[1]), BF16) for w in weights]
    return pl.pallas_call(
        functools.partial(_rms_proj_kernel, n_out),
        grid=(t // tm,),
        in_specs=in_specs,
        out_specs=out_specs,
        out_shape=out_shape,
        compiler_params=_params(("parallel",)),
        name="rms_proj",
    )(x, *[g.reshape(1, d) for g in gains], *weights)


SB_QBLKS = 4
SB_ROWS = SB_QBLKS * BLK
SB_PAIRS = 2
SB_BAND = 3
SB_UNDERFLOW = 106.0


def _sb_kernel(q_ref, k_ref, v_ref, tri_ref, o_ref, carry_ref, acc_ref):
    qt = pl.program_id(2)
    head0 = _head0_lanes()
    tri = tri_ref[...]
    pairs = range(SB_PAIRS)
    q_all = [q_ref[:, p * LANES:(p + 1) * LANES] * QK_SCALE for p in pairs]

    def key_block(p, k0, row_lo, row_hi, diagonal, carry, acc, live_from=None):
        cols = slice(p * LANES, (p + 1) * LANES)
        kst = _split_heads(k_ref[pl.ds(k0, BLK), cols], head0)
        vst = _split_heads(v_ref[pl.ds(k0, BLK), cols], head0)
        z = lax.dot_general(q_all[p][row_lo:row_hi], kst, NT_DIMS, preferred_element_type=F32)
        sp = jnp.maximum(z, 0.0) + jnp.log(1.0 + jnp.exp2(jnp.abs(z) * (-LOG2E)))
        if diagonal:
            t_rel = lax.broadcasted_iota(jnp.int32, (BLK, 2 * BLK), 0)
            s_rel = lax.broadcasted_iota(jnp.int32, (BLK, 2 * BLK), 1) & (BLK - 1)
            causal = s_rel < t_rel

            def mask(t):
                top = jnp.where(causal, t[:BLK], 0.0)
                return top if t.shape[0] == BLK else jnp.concatenate([top, t[BLK:]], axis=0)
        elif live_from is not None:
            live = lax.broadcasted_iota(jnp.int32, z.shape, 0) + row_lo >= live_from

            def mask(t):
                return jnp.where(live, t, 0.0)
        else:
            def mask(t):
                return t

        sp = mask(sp)
        hi = sp.astype(BF16)
        lo = (sp - hi.astype(F32)).astype(BF16)
        sums = [jnp.dot(jnp.concatenate([hi[:, h * BLK:(h + 1) * BLK], lo[:, h * BLK:(h + 1) * BLK]], axis=1),
                        tri, preferred_element_type=F32) for h in range(2)]
        suffix = jnp.concatenate([s[:, :BLK] for s in sums], axis=1)
        total = jnp.concatenate([s[:, BLK:] for s in sums], axis=1)
        w = mask(jnp.exp2((z - (suffix + carry[row_lo:row_hi])) * LOG2E))
        pv = jnp.dot(w.astype(BF16), vst, preferred_element_type=F32)

        def all_rows(t):
            parts = [jnp.zeros((n, t.shape[1]), F32) if n else None for n in (row_lo, SB_ROWS - row_hi)]
            parts = [x for x in (parts[0], t, parts[1]) if x is not None]
            return t if len(parts) == 1 else jnp.concatenate(parts, axis=0)

        return carry + all_rows(total), acc + all_rows(pv)

    r0 = qt * SB_ROWS

    def band(lowest, width):
        carry = [jnp.zeros((SB_ROWS, 2 * BLK), F32) for _ in pairs]
        acc = [jnp.zeros((SB_ROWS, LANES), F32) for _ in pairs]
        for j in range(SB_QBLKS - 1, lowest - 1, -1):
            for p in pairs:
                carry[p], acc[p] = key_block(p, pl.multiple_of(r0 + j * BLK, BLK), max(j, 0) * BLK,
                                             min(j + width, SB_QBLKS) * BLK, j >= 0, carry[p], acc[p])
        for p in pairs:
            carry_ref[p] = carry[p]
            acc_ref[p] = acc[p]

    @pl.when(qt == 0)
    def _():
        band(0, SB_QBLKS)

    @pl.when(qt > 0)
    def _():
        band(1 - SB_BAND, SB_BAND)

    top = SB_QBLKS - 1 - SB_BAND
    n_steps = jnp.where(qt == 0, 0, top + 1 + qt * SB_QBLKS)

    def step(state):
        j = top - state[0]
        lowest = None
        for p in pairs:
            carry, acc = key_block(p, pl.multiple_of(r0 + j * BLK, BLK), 0, SB_ROWS, False, carry_ref[p], acc_ref[p],
                                   live_from=(j + SB_BAND) * BLK)
            carry_ref[p] = carry
            acc_ref[p] = acc
            lowest = jnp.min(carry) if lowest is None else jnp.minimum(lowest, jnp.min(carry))
        return state[0] + 1, lowest

    lax.while_loop(lambda s: jnp.logical_and(s[0] < n_steps, s[1] < SB_UNDERFLOW), step,
                   (jnp.int32(0), jnp.min(carry_ref[...])))
    for p in pairs:
        o_ref[:, p * LANES:(p + 1) * LANES] = acc_ref[p].astype(o_ref.dtype)


def _sb_tri_weights():
    j = jnp.arange(2 * BLK)[:, None] % BLK
    c = jnp.arange(2 * BLK)[None, :]
    return ((c >= BLK) | (j >= c)).astype(BF16)


def _sb_attn(proj, batch, s_len):
    width = SB_PAIRS * LANES
    n_groups = SB_WIDTH // width
    n_qt = s_len // SB_ROWS
    kv_blk = (s_len, width)
    return pl.pallas_call(
        _sb_kernel,
        grid=(batch, n_groups, n_qt),
        in_specs=[
            pl.BlockSpec((SB_ROWS, width), lambda b, p, t: (b * n_qt + t, p)),
            pl.BlockSpec(kv_blk, lambda b, p, t: (b, n_groups + p)),
            pl.BlockSpec(kv_blk, lambda b, p, t: (b, 2 * n_groups + p)),
            _resident((2 * BLK, 2 * BLK)),
        ],
        out_specs=pl.BlockSpec((SB_ROWS, width), lambda b, p, t: (b * n_qt + t, p)),
        out_shape=jax.ShapeDtypeStruct((batch * s_len, SB_WIDTH), BF16),
        scratch_shapes=[pltpu.VMEM((SB_PAIRS, SB_ROWS, 2 * BLK), F32), pltpu.VMEM((SB_PAIRS, SB_ROWS, LANES), F32)],
        compiler_params=_params(("parallel", "parallel", "arbitrary")),
        name="sb_attn",
    )(proj, proj, proj, _sb_tri_weights())


MEM_ROWS = 512


def _mem_heads(q_ref, kv_ref, o_ref):
    s_len = q_ref.shape[0]
    mem_len = kv_ref.shape[0]
    head0 = _head0_lanes()
    n_pairs = MEM_WIDTH // LANES
    ksts = [_split_heads(kv_ref[:, p * LANES:(p + 1) * LANES], head0) for p in range(n_pairs)]
    vsts = [_with_den_cols(_split_heads(kv_ref[:, MEM_WIDTH + p * LANES:MEM_WIDTH + (p + 1) * LANES], head0))
            for p in range(n_pairs)]

    def step(c, carry):
        rows = pl.ds(pl.multiple_of(c * MEM_ROWS, MEM_ROWS), MEM_ROWS)
        for p in range(n_pairs):
            q = q_ref[rows, p * LANES:(p + 1) * LANES] * QK_SCALE
            z = lax.dot_general(q, ksts[p], NT_DIMS, preferred_element_type=F32)
            es = []
            for h in range(2):
                zh = z[:, h * mem_len:(h + 1) * mem_len]
                es.append(jnp.exp(zh - jnp.max(zh, axis=-1, keepdims=True)))
            od = jnp.dot(jnp.concatenate(es, axis=1).astype(BF16), vsts[p], preferred_element_type=F32)
            o_ref[rows, p * LANES:(p + 1) * LANES] = (od[:, :LANES] / od[:, LANES:]).astype(o_ref.dtype)
        return carry

    lax.fori_loop(0, s_len // MEM_ROWS, step, 0)


def _mem_attn(proj, q_col_block, kv_mem, batch, s_len):
    mem_len = kv_mem.shape[0] // batch
    return pl.pallas_call(
        _mem_heads,
        grid=(batch,),
        in_specs=[
            pl.BlockSpec((s_len, MEM_WIDTH), lambda b: (b, q_col_block)),
            pl.BlockSpec((mem_len, 2 * MEM_WIDTH), lambda b: (b, 0)),
        ],
        out_specs=pl.BlockSpec((s_len, MEM_WIDTH), lambda b: (b, 0)),
        out_shape=jax.ShapeDtypeStruct((batch * s_len, MEM_WIDTH), BF16),
        compiler_params=_params(("parallel",)),
        name="mem_attn",
    )(proj, kv_mem)


def _proj_b_kernel(x_ref, gkv_ref, gq_ref, wq_ref, wkv_ref, o0, o1, o2, stage_ref):
    tm = x_ref.shape[0]
    s_len = o0.shape[0]
    c = pl.program_id(1)
    gw = GROUP_WIDTH
    xhat = _rms_scale(x_ref[...])
    q = jnp.dot((xhat * gq_ref[...]).astype(BF16), wq_ref[...], preferred_element_type=F32)
    kv = jnp.dot((xhat * gkv_ref[...]).astype(BF16), wkv_ref[...], preferred_element_type=F32)
    for g, ((_, d), o_ref) in enumerate(zip(DIL_GROUPS, (o0, o1, o2))):
        qkv = jnp.concatenate([q[:, g * gw:(g + 1) * gw], kv[:, g * gw:(g + 1) * gw],
                               kv[:, DIL_WIDTH + g * gw:DIL_WIDTH + (g + 1) * gw]], axis=1)
        if d == 1:
            rows = pl.ds(pl.multiple_of(c * tm, tm), tm)
            o_ref[rows, :3 * gw] = qkv.astype(BF16)
            o_ref[rows, 3 * gw:] = q[:, DIL_WIDTH:].astype(BF16)
        else:
            n_cols = 3 * gw // LANES
            for j in range(n_cols):
                stage_ref[j] = qkv[:, j * LANES:(j + 1) * LANES]
            n = tm // d
            for r in range(d):
                dst = pl.ds(pl.multiple_of(r * (s_len // d) + c * n, n), n)
                o_ref[dst, :] = jnp.concatenate(
                    [stage_ref[j, pl.ds(r, n, stride=d), :] for j in range(n_cols)], axis=1).astype(BF16)


def _proj_b(h, g_kv, g_q, w_kv, w_in, batch, s_len):
    d = h.shape[1]
    tm = PROJ_ROWS
    n_tiles = s_len // tm
    widths = [3 * GROUP_WIDTH + MEM_WIDTH, 3 * GROUP_WIDTH, 3 * GROUP_WIDTH]
    return pl.pallas_call(
        _proj_b_kernel,
        grid=(batch, n_tiles),
        in_specs=[pl.BlockSpec((tm, d), lambda b, c: (b * n_tiles + c, 0)), _resident((1, d)), _resident((1, d)),
                  _resident(w_in.shape), _resident(w_kv.shape)],
        out_specs=[pl.BlockSpec((s_len, n), lambda b, c: (b, 0)) for n in widths],
        out_shape=[jax.ShapeDtypeStruct((batch * s_len, n), BF16) for n in widths],
        scratch_shapes=[pltpu.VMEM((3 * GROUP_WIDTH // LANES, tm, LANES), F32)],
        compiler_params=_params(("parallel", "arbitrary")),
        name="proj_b",
    )(h, g_kv.reshape(1, d), g_q.reshape(1, d), w_in.astype(BF16), w_kv.astype(BF16))


COMBINE_ROWS = 256


DIL_BLOCKS_PER_ITER = 4


def _softmax_parts(parts):
    m = parts[0]
    for t in parts[1:]:
        m = jnp.maximum(m, t)
    m = jnp.max(m, axis=-1, keepdims=True)
    return [jnp.exp(t - m) for t in parts], m


def _cross_attn_kernel(qkv0, qkv1, qkv2, kvm_ref, bias_first_ref, bias_ref, y0, y1, y2, om_ref,
                       o_cls, lse_cls, stage):
    s_len = qkv0.shape[0]
    head0 = _head0_lanes()
    gw = GROUP_WIDTH

    def block(g, qkv, p, row0, first):
        qc = slice(p * LANES, (p + 1) * LANES)
        kc = slice(gw + p * LANES, gw + (p + 1) * LANES)
        vc = slice(2 * gw + p * LANES, 2 * gw + (p + 1) * LANES)
        cur = pl.ds(pl.multiple_of(row0, BLK), BLK)
        q = qkv[cur, qc] * QK_SCALE
        kst = _split_heads(qkv[cur, kc], head0)
        vst = _with_den_cols(_split_heads(qkv[cur, vc], head0))
        if first:
            z = lax.dot_general(q, kst, NT_DIMS, preferred_element_type=F32) + bias_first_ref[g, p]
            heads = [[z[:, h * BLK:(h + 1) * BLK]] for h in range(2)]
        else:
            prev = pl.ds(pl.multiple_of(row0 - BLK, BLK), BLK)
            kst = jnp.concatenate([_split_heads(qkv[prev, kc], head0), kst], axis=0)
            vst = jnp.concatenate([_with_den_cols(_split_heads(qkv[prev, vc], head0)), vst], axis=0)
            z = lax.dot_general(q, kst, NT_DIMS, preferred_element_type=F32) + bias_ref[g, p]
            heads = [[z[:, h * BLK:(h + 1) * BLK], z[:, (2 + h) * BLK:(3 + h) * BLK]] for h in range(2)]
        (e0, m0), (e1, m1) = [_softmax_parts(parts) for parts in heads]
        pmat = jnp.concatenate([x for pair in zip(e0, e1) for x in pair], axis=1).astype(BF16)
        od = jnp.dot(pmat, vst, preferred_element_type=F32)
        o_cls[g, cur, qc] = od[:, :LANES] / od[:, LANES:]
        lse_cls[g, cur, qc] = jnp.where(head0, m0, m1) + jnp.log(od[:, LANES:])

    for g, ((_, d), qkv) in enumerate(zip(DIL_GROUPS, (qkv0, qkv1, qkv2))):
        blocks_per_class = s_len // d // BLK
        n_iters = s_len // BLK // DIL_BLOCKS_PER_ITER

        def blocks(it, starts_class, g=g, qkv=qkv, blocks_per_class=blocks_per_class):
            for k in range(DIL_BLOCKS_PER_ITER):
                first = blocks_per_class == 1 or (k == 0 and starts_class)
                for p in range(gw // LANES):
                    block(g, qkv, p, (it * DIL_BLOCKS_PER_ITER + k) * BLK, first)

        if blocks_per_class > DIL_BLOCKS_PER_ITER:
            blocks(0, True)
            lax.fori_loop(1, n_iters, lambda it, x, blocks=blocks: (blocks(it, False), x)[1], 0)
        else:
            lax.fori_loop(0, n_iters, lambda it, x, blocks=blocks: (blocks(it, True), x)[1], 0)

    def natural_rows(src, g, c, k):
        d = DIL_GROUPS[g][1]
        if d == 1:
            return src[g, pl.ds(pl.multiple_of(c * COMBINE_ROWS, COMBINE_ROWS), COMBINE_ROWS), :]
        n = COMBINE_ROWS // d
        n_cols = gw // LANES
        for r in range(d):
            rows = pl.ds(pl.multiple_of(r * (s_len // d) + c * n, n), n)
            for j in range(n_cols):
                stage[k * n_cols + j, pl.ds(r, n, stride=d), :] = src[g, rows, j * LANES:(j + 1) * LANES]
        return jnp.concatenate([stage[k * n_cols + j] for j in range(n_cols)], axis=1)

    def combine(c, carry):
        ls = [natural_rows(lse_cls, g, c, g) for g in range(3)]
        m = jnp.maximum(jnp.maximum(ls[0], ls[1]), ls[2])
        es = [jnp.exp(l - m) for l in ls]
        tot = es[0] + es[1] + es[2]
        rows = pl.ds(pl.multiple_of(c * COMBINE_ROWS, COMBINE_ROWS), COMBINE_ROWS)
        for g, y_ref in enumerate((y0, y1, y2)):
            y_ref[rows, :] = (natural_rows(o_cls, g, c, 3 + g) * (es[g] / tot)).astype(y_ref.dtype)
        return carry

    lax.fori_loop(0, s_len // COMBINE_ROWS, combine, 0)
    _mem_heads(qkv0.at[:, 3 * gw:], kvm_ref, om_ref)


def _dil_biases(slopes):
    i = jnp.arange(BLK)[:, None]
    j = jnp.arange(2 * BLK)[None, :]
    delta = i + BLK - j
    valid = (delta >= 0) & (delta <= BLK)
    firsts, others = [], []
    for g, (_, d) in enumerate(DIL_GROUPS):
        dist = (delta * d).astype(F32)
        f_p, o_p = [], []
        for p in range(2):
            halves = [jnp.where(valid, -slopes[g * HEADS_PER_GROUP + 2 * p + h] * dist, -jnp.inf) for h in range(2)]
            f_p.append(jnp.concatenate([halves[0][:, BLK:], halves[1][:, BLK:]], axis=1))
            o_p.append(jnp.concatenate([halves[0][:, :BLK], halves[1][:, :BLK],
                                        halves[0][:, BLK:], halves[1][:, BLK:]], axis=1))
        firsts.append(jnp.stack(f_p))
        others.append(jnp.stack(o_p))
    return jnp.stack(firsts), jnp.stack(others)


def _cross_attn(qkvs, kv_mem, batch, s_len):
    mem_len = kv_mem.shape[0] // batch
    slopes = 2.0 ** (-ALIBI_MAX_BIAS * jnp.arange(1, N_DIL_HEADS + 1, dtype=F32) / N_DIL_HEADS)
    bias_first, bias = _dil_biases(slopes)
    seq = lambda n: pl.BlockSpec((s_len, n), lambda b: (b, 0))
    n_out = len(DIL_GROUPS) + 1
    return pl.pallas_call(
        _cross_attn_kernel,
        grid=(batch,),
        in_specs=[seq(a.shape[1]) for a in qkvs]
        + [pl.BlockSpec((mem_len, 2 * MEM_WIDTH), lambda b: (b, 0)), _resident(bias_first.shape), _resident(bias.shape)],
        out_specs=[seq(GROUP_WIDTH)] * n_out,
        out_shape=[jax.ShapeDtypeStruct((batch * s_len, GROUP_WIDTH), BF16)] * n_out,
        scratch_shapes=[pltpu.VMEM((len(DIL_GROUPS), s_len, GROUP_WIDTH), F32)] * 2
        + [pltpu.VMEM((2 * len(DIL_GROUPS) * GROUP_WIDTH // LANES, COMBINE_ROWS, LANES), F32)],
        compiler_params=_params(("parallel",)),
        name="cross_attn",
    )(*qkvs, kv_mem, bias_first, bias)


SUBLANES = 8
FFN_ROWS = 512
FFN_CHUNK = 256
FFN_VREG_ROWS = FFN_ROWS // SUBLANES
FFN_PITCH = FFN_VREG_ROWS + SUBLANES


def _out_ffn_kernel(n_attn, tiles_per_seq, final, h_ref, *refs):
    a_refs, wo_refs = refs[:n_attn], refs[n_attn:2 * n_attn]
    g_ref, wup_ref, wconv_ref, wdown_ref = refs[2 * n_attn:2 * n_attn + 4]
    rest = refs[2 * n_attn + 4:]
    if final:
        gf_ref, out_ref, stage_ref, xe_ref, act_ref, tail_ref = rest
    else:
        out_ref, stage_ref, xe_ref, act_ref, tail_ref = rest
    n_cols = h_ref.shape[1] // LANES
    nv = FFN_VREG_ROWS
    i = pl.program_id(0)

    @pl.when(i == 0)
    def _():
        tail_ref[...] = jnp.zeros_like(tail_ref)

    h = h_ref[...]
    for a_ref, wo_ref in zip(a_refs, wo_refs):
        h = h + jnp.dot(a_ref[...], wo_ref[...], preferred_element_type=F32)
    xn = _rms_scale(h) * g_ref[...]

    for c in range(n_cols):
        for s in range(SUBLANES):
            stage_ref[c, s * FFN_PITCH:s * FFN_PITCH + nv, :] = xn[s * nv:(s + 1) * nv, c * LANES:(c + 1) * LANES]
    for jj in range(nv * SUBLANES // BF16_ROWS):
        vregs = [jnp.concatenate([stage_ref[c, pl.ds(j, SUBLANES, stride=FFN_PITCH), :] for c in range(n_cols)], axis=1)
                 for j in range(jj * BF16_ROWS // SUBLANES, (jj + 1) * BF16_ROWS // SUBLANES)]
        xe_ref[jj * BF16_ROWS:(jj + 1) * BF16_ROWS, :] = jnp.concatenate(vregs, axis=0).astype(xe_ref.dtype)

    starts_seq = i % tiles_per_seq == 0
    first_sublane = lax.broadcasted_iota(jnp.int32, (SUBLANES, FFN_CHUNK), 0) == 0

    def conv(u, wc, tail):
        def before_first(prev_vreg, last_vreg):
            return jnp.where(first_sublane, pltpu.roll(prev_vreg, 1, axis=0), pltpu.roll(last_vreg, 1, axis=0))

        back1 = before_first(tail[SUBLANES:], u[-SUBLANES:])
        back2 = before_first(tail[:SUBLANES], u[-2 * SUBLANES:-SUBLANES])
        u1 = jnp.concatenate([back1, u[:-SUBLANES]], axis=0)
        u2 = jnp.concatenate([back2, back1, u[:-2 * SUBLANES]], axis=0)
        return wc[0:1] * u2 + wc[1:2] * u1 + wc[2:3] * u

    xe = xe_ref[...]
    for c in range(D_FF // FFN_CHUNK):
        convs = []
        for part in range(2):
            cols = slice(part * D_FF + c * FFN_CHUNK, part * D_FF + (c + 1) * FFN_CHUNK)
            u = jnp.dot(xe, wup_ref[:, cols], preferred_element_type=F32)
            tail = tail_ref[2 * c + part]
            tail_ref[2 * c + part] = u[-2 * SUBLANES:]
            convs.append(conv(u, wconv_ref[:, cols], jnp.where(starts_seq, jnp.zeros_like(tail), tail)))
        half = 0.5 * convs[1]
        act_ref[:, c * FFN_CHUNK:(c + 1) * FFN_CHUNK] = ((half + half * jnp.tanh(half)) * convs[0]).astype(act_ref.dtype)

    y = jnp.dot(act_ref[...], wdown_ref[...], preferred_element_type=F32)
    for c in range(n_cols):
        for j in range(nv):
            stage_ref[c, pl.ds(j, SUBLANES, stride=FFN_PITCH), :] = y[j * SUBLANES:(j + 1) * SUBLANES,
                                                                      c * LANES:(c + 1) * LANES]
    y = h + jnp.concatenate(
        [jnp.concatenate([stage_ref[c, s * FFN_PITCH:s * FFN_PITCH + nv, :] for c in range(n_cols)], axis=1)
         for s in range(SUBLANES)], axis=0)
    if final:
        y = _rms_scale(y) * gf_ref[...]
    out_ref[...] = y


def _out_ffn(h, attn_parts, w_out, g_ffn, w_up, w_conv, w_down, s_len, g_final=None):
    t, d = h.shape
    tm = FFN_ROWS
    n_attn = len(attn_parts)
    final = g_final is not None
    offs = [0]
    for a in attn_parts:
        offs.append(offs[-1] + a.shape[1])
    wo_parts = [w_out[offs[k]:offs[k + 1]].astype(BF16) for k in range(n_attn)]
    row = lambda n: pl.BlockSpec((tm, n), lambda i: (i, 0))
    in_specs = [row(d)] + [row(a.shape[1]) for a in attn_parts] + [_resident(w.shape) for w in wo_parts]
    in_specs += [_resident((1, d)), _resident(w_up.shape), _resident(w_conv.shape), _resident(w_down.shape)]
    args = [h, *attn_parts, *wo_parts, g_ffn.reshape(1, d), w_up.astype(BF16), w_conv, w_down.astype(BF16)]
    if final:
        in_specs.append(_resident((1, d)))
        args.append(g_final.reshape(1, d))
    return pl.pallas_call(
        functools.partial(_out_ffn_kernel, n_attn, s_len // tm, final),
        grid=(t // tm,),
        in_specs=in_specs,
        out_specs=row(d),
        out_shape=jax.ShapeDtypeStruct((t, d), F32),
        scratch_shapes=[pltpu.VMEM((d // LANES, SUBLANES * FFN_PITCH, LANES), F32), pltpu.VMEM((tm, d), BF16),
                        pltpu.VMEM((tm, D_FF), BF16),
                        pltpu.VMEM((2 * D_FF // FFN_CHUNK, 2 * SUBLANES, FFN_CHUNK), F32)],
        compiler_params=_params(("arbitrary",)),
        name="out_ffn",
    )(*args)


def kernel(x, mem, a_norm_attn, a_w_in, a_w_out, a_norm_mem, a_w_mem_kv, a_norm_ffn, a_ffn_up, a_ffn_conv, a_ffn_down, kv_norm, w_kv_shared, b_norm_attn, b_w_in, b_w_out, b_norm_mem, b_w_mem_kv, b_norm_ffn, b_ffn_up, b_ffn_conv, b_ffn_down, final_norm):
    batch, s_len, d = x.shape
    assert a_w_in.shape[0] == 1 and b_w_in.shape[0] == 1, "one self-decoder and one cross-decoder layer"
    assert d == D_MODEL and s_len % (BLK * DIL_GROUPS[-1][1]) == 0 and s_len % FFN_ROWS == 0
    t = batch * s_len
    h = x.reshape(t, d)
    mem2 = mem.reshape(batch * mem.shape[1], d)

    kvm_a, kvm_b = _rms_proj(mem2, [a_norm_mem[0], b_norm_mem[0]],
                             [a_w_mem_kv[0].astype(BF16), b_w_mem_kv[0].astype(BF16)], PROJ_ROWS)

    (proj_a,) = _rms_proj(h, [a_norm_attn[0]], [a_w_in[0].astype(BF16)], PROJ_ROWS)
    o_sb = _sb_attn(proj_a, batch, s_len)
    o_mem = _mem_attn(proj_a, 3 * SB_WIDTH // MEM_WIDTH, kvm_a, batch, s_len)
    h = _out_ffn(h, [o_sb, o_mem], a_w_out[0], a_norm_ffn[0], a_ffn_up[0], a_ffn_conv[0], a_ffn_down[0], s_len)

    qkvs = _proj_b(h, kv_norm, b_norm_attn[0], w_kv_shared, b_w_in[0], batch, s_len)
    attn_parts = _cross_attn(qkvs, kvm_b, batch, s_len)
    h = _out_ffn(h, attn_parts, b_w_out[0], b_norm_ffn[0], b_ffn_up[0], b_ffn_conv[0], b_ffn_down[0], s_len,
                 g_final=final_norm)
    return h.reshape(batch, s_len, d)
```

```python
import functools

import jax
import jax.numpy as jnp
from jax import lax
from jax.experimental import pallas as pl
from jax.experimental.pallas import tpu as pltpu

D_MODEL = 1024
HEAD_DIM = 64
N_SB_HEADS = 12
N_MEM_HEADS = 4
DIL_GROUPS = ((128, 1), (512, 4), (2048, 16))
HEADS_PER_GROUP = 4
N_DIL_HEADS = HEADS_PER_GROUP * len(DIL_GROUPS)
SB_WIDTH = N_SB_HEADS * HEAD_DIM
MEM_WIDTH = N_MEM_HEADS * HEAD_DIM
DIL_WIDTH = N_DIL_HEADS * HEAD_DIM
GROUP_WIDTH = HEADS_PER_GROUP * HEAD_DIM
D_FF = 2816
CONV_WIDTH = 3
EPS = 1e-6
ALIBI_MAX_BIAS = 8.0
QK_SCALE = HEAD_DIM ** -0.5
LOG2E = 1.4426950408889634

LANES = 128
BF16_ROWS = 16
VMEM_LIMIT_BYTES = 56 * 1024 * 1024

BLK = 128
F32 = jnp.float32
BF16 = jnp.bfloat16
NT_DIMS = (((1,), (1,)), ((), ()))


def _params(semantics):
    return pltpu.CompilerParams(dimension_semantics=semantics, vmem_limit_bytes=VMEM_LIMIT_BYTES)


def _resident(shape):
    return pl.BlockSpec(shape, lambda *_: (0,) * len(shape), pipeline_mode=pl.Buffered(1))


def _head0_lanes():
    return lax.broadcasted_iota(jnp.int32, (1, LANES), 1) < HEAD_DIM


def _split_heads(t, head0):
    zero = jnp.zeros_like(t)
    return jnp.concatenate([jnp.where(head0, t, zero), jnp.where(head0, zero, t)], axis=0)


def _with_den_cols(vst):
    row = lax.broadcasted_iota(jnp.int32, vst.shape, 0)
    lane = lax.broadcasted_iota(jnp.int32, vst.shape, 1)
    owns = (row < vst.shape[0] // 2) == (lane < HEAD_DIM)
    return jnp.concatenate([vst, jnp.where(owns, 1.0, 0.0).astype(vst.dtype)], axis=1)


def _rms_scale(x):
    return x * lax.rsqrt(jnp.mean(x * x, axis=-1, keepdims=True) + EPS)


PROJ_ROWS = 512


def _rms_proj_kernel(n_out, x_ref, *refs):
    g_refs, w_refs, o_refs = refs[:n_out], refs[n_out:2 * n_out], refs[2 * n_out:]
    xhat = _rms_scale(x_ref[...])
    for g_ref, w_ref, o_ref in zip(g_refs, w_refs, o_refs):
        xn = (xhat * g_ref[...]).astype(BF16)
        o_ref[...] = jnp.dot(xn, w_ref[...], preferred_element_type=F32).astype(o_ref.dtype)


def _rms_proj(x, gains, weights, tm):
    t, d = x.shape
    n_out = len(gains)
    in_specs = [pl.BlockSpec((tm, d), lambda i: (i, 0))]
    in_specs += [_resident((1, d)) for _ in gains]
    in_specs += [_resident(w.shape) for w in weights]
    out_specs = [pl.BlockSpec((tm, w.shape[1]), lambda i: (i, 0)) for w in weights]
    out_shape = [jax.ShapeDtypeStruct((t, w.shape[1]), BF16) for w in weights]
    return pl.pallas_call(
        functools.partial(_rms_proj_kernel, n_out),
        grid=(t // tm,),
        in_specs=in_specs,
        out_specs=out_specs,
        out_shape=out_shape,
        compiler_params=_params(("parallel",)),
        name="rms_proj",
    )(x, *[g.reshape(1, d) for g in gains], *weights)


SB_QBLKS = 4
SB_ROWS = SB_QBLKS * BLK
SB_PAIRS = 2
SB_BAND = 3
SB_UNDERFLOW = 106.0


def _sb_kernel(q_ref, k_ref, v_ref, tri_ref, o_ref, carry_ref, acc_ref, lowest_ref):
    qt = pl.program_id(2)
    head0 = _head0_lanes()
    tri = tri_ref[...]
    pairs = range(SB_PAIRS)
    q_all = [q_ref[:, p * LANES:(p + 1) * LANES] * QK_SCALE for p in pairs]

    def key_block(p, k0, row_lo, row_hi, diagonal, carry, acc, live_from=None):
        cols = slice(p * LANES, (p + 1) * LANES)
        kst = _split_heads(k_ref[pl.ds(k0, BLK), cols], head0)
        vst = _split_heads(v_ref[pl.ds(k0, BLK), cols], head0)
        z = lax.dot_general(q_all[p][row_lo:row_hi], kst, NT_DIMS, preferred_element_type=F32)
        sp = jnp.maximum(z, 0.0) + jnp.log(1.0 + jnp.exp2(jnp.abs(z) * (-LOG2E)))
        if diagonal:
            t_rel = lax.broadcasted_iota(jnp.int32, (BLK, 2 * BLK), 0)
            s_rel = lax.broadcasted_iota(jnp.int32, (BLK, 2 * BLK), 1) & (BLK - 1)
            causal = s_rel < t_rel

            def mask(t):
                top = jnp.where(causal, t[:BLK], 0.0)
                return top if t.shape[0] == BLK else jnp.concatenate([top, t[BLK:]], axis=0)
        elif live_from is not None:
            live = lax.broadcasted_iota(jnp.int32, z.shape, 0) + row_lo >= live_from

            def mask(t):
                return jnp.where(live, t, 0.0)
        else:
            def mask(t):
                return t

        sp = mask(sp)
        sp16 = sp.astype(BF16)
        sums = [jnp.dot(sp16[:, h * BLK:(h + 1) * BLK], tri, preferred_element_type=F32)
                for h in range(2)]
        suffix = jnp.concatenate([s[:, :BLK] for s in sums], axis=1)
        total = jnp.concatenate([s[:, BLK:] for s in sums], axis=1)
        w = mask(jnp.exp2((z - (suffix + carry[row_lo:row_hi])) * LOG2E))
        pv = jnp.dot(w.astype(BF16), vst, preferred_element_type=F32)

        def all_rows(t):
            parts = [jnp.zeros((n, t.shape[1]), F32) if n else None for n in (row_lo, SB_ROWS - row_hi)]
            parts = [x for x in (parts[0], t, parts[1]) if x is not None]
            return t if len(parts) == 1 else jnp.concatenate(parts, axis=0)

        return carry + all_rows(total), acc + all_rows(pv)

    r0 = qt * SB_ROWS

    def band(lowest, width):
        carry = [jnp.zeros((SB_ROWS, 2 * BLK), F32) for _ in pairs]
        acc = [jnp.zeros((SB_ROWS, LANES), F32) for _ in pairs]
        for j in range(SB_QBLKS - 1, lowest - 1, -1):
            for p in pairs:
                carry[p], acc[p] = key_block(p, pl.multiple_of(r0 + j * BLK, BLK), max(j, 0) * BLK,
                                             min(j + width, SB_QBLKS) * BLK, j >= 0, carry[p], acc[p])
        lowest = None
        for p in pairs:
            carry_ref[p] = carry[p]
            acc_ref[p] = acc[p]
            o_ref[:, p * LANES:(p + 1) * LANES] = acc[p].astype(o_ref.dtype)
            lowest = jnp.min(carry[p]) if lowest is None else jnp.minimum(lowest, jnp.min(carry[p]))
        lowest_ref[0] = lowest

    @pl.when(qt == 0)
    def _():
        band(0, SB_QBLKS)

    @pl.when(qt > 0)
    def _():
        band(1 - SB_BAND, SB_BAND)

    top = SB_QBLKS - 1 - SB_BAND
    n_steps = top + 1 + qt * SB_QBLKS

    def step(state):
        j = top - state[0]
        lowest = None
        for p in pairs:
            carry, acc = key_block(p, pl.multiple_of(r0 + j * BLK, BLK), 0, SB_ROWS, False, carry_ref[p], acc_ref[p],
                                   live_from=(j + SB_BAND) * BLK)
            carry_ref[p] = carry
            acc_ref[p] = acc
            lowest = jnp.min(carry) if lowest is None else jnp.minimum(lowest, jnp.min(carry))
        return state[0] + 1, lowest

    @pl.when(jnp.logical_and(qt > 0, lowest_ref[0] < SB_UNDERFLOW))
    def _():
        lax.while_loop(lambda s: jnp.logical_and(s[0] < n_steps, s[1] < SB_UNDERFLOW), step,
                       (jnp.int32(0), lowest_ref[0]))
        for p in pairs:
            o_ref[:, p * LANES:(p + 1) * LANES] = acc_ref[p].astype(o_ref.dtype)


def _sb_tri_weights():
    j = jnp.arange(BLK)[:, None]
    c = jnp.arange(2 * BLK)[None, :]
    return ((c >= BLK) | (j >= c)).astype(BF16)


def _sb_attn(proj, batch, s_len):
    width = SB_PAIRS * LANES
    n_groups = SB_WIDTH // width
    n_qt = s_len // SB_ROWS
    kv_blk = (s_len, width)
    return pl.pallas_call(
        _sb_kernel,
        grid=(batch, n_groups, n_qt),
        in_specs=[
            pl.BlockSpec((SB_ROWS, width), lambda b, p, t: (b * n_qt + t, p)),
            pl.BlockSpec(kv_blk, lambda b, p, t: (b, n_groups + p)),
            pl.BlockSpec(kv_blk, lambda b, p, t: (b, 2 * n_groups + p)),
            _resident((BLK, 2 * BLK)),
        ],
        out_specs=pl.BlockSpec((SB_ROWS, width), lambda b, p, t: (b * n_qt + t, p)),
        out_shape=jax.ShapeDtypeStruct((batch * s_len, SB_WIDTH), BF16),
        scratch_shapes=[pltpu.VMEM((SB_PAIRS, SB_ROWS, 2 * BLK), F32), pltpu.VMEM((SB_PAIRS, SB_ROWS, LANES), F32),
                        pltpu.SMEM((1,), F32)],
        compiler_params=_params(("parallel", "parallel", "arbitrary")),
        name="sb_attn",
    )(proj, proj, proj, _sb_tri_weights())


MEM_ROWS = 512


def _mem_heads(q_ref, kv_ref, o_ref):
    s_len = q_ref.shape[0]
    mem_len = kv_ref.shape[0]
    head0 = _head0_lanes()
    n_pairs = MEM_WIDTH // LANES
    ksts = [_split_heads(kv_ref[:, p * LANES:(p + 1) * LANES], head0) for p in range(n_pairs)]
    vsts = [_with_den_cols(_split_heads(kv_ref[:, MEM_WIDTH + p * LANES:MEM_WIDTH + (p + 1) * LANES], head0))
            for p in range(n_pairs)]

    def step(c, carry):
        rows = pl.ds(pl.multiple_of(c * MEM_ROWS, MEM_ROWS), MEM_ROWS)
        for p in range(n_pairs):
            q = q_ref[rows, p * LANES:(p + 1) * LANES] * QK_SCALE
            z = lax.dot_general(q, ksts[p], NT_DIMS, preferred_element_type=F32)
            es = []
            for h in range(2):
                zh = z[:, h * mem_len:(h + 1) * mem_len]
                es.append(jnp.exp(zh - jnp.max(zh, axis=-1, keepdims=True)))
            od = jnp.dot(jnp.concatenate(es, axis=1).astype(BF16), vsts[p], preferred_element_type=F32)
            o_ref[rows, p * LANES:(p + 1) * LANES] = (od[:, :LANES] / od[:, LANES:]).astype(o_ref.dtype)
        return carry

    lax.fori_loop(0, s_len // MEM_ROWS, step, 0)


def _mem_attn(proj, q_col_block, kv_mem, batch, s_len):
    mem_len = kv_mem.shape[0] // batch
    return pl.pallas_call(
        _mem_heads,
        grid=(batch,),
        in_specs=[
            pl.BlockSpec((s_len, MEM_WIDTH), lambda b: (b, q_col_block)),
            pl.BlockSpec((mem_len, 2 * MEM_WIDTH), lambda b: (b, 0)),
        ],
        out_specs=pl.BlockSpec((s_len, MEM_WIDTH), lambda b: (b, 0)),
        out_shape=jax.ShapeDtypeStruct((batch * s_len, MEM_WIDTH), BF16),
        compiler_params=_params(("parallel",)),
        name="mem_attn",
    )(proj, kv_mem)


def _proj_b_kernel(x_ref, gkv_ref, gq_ref, wq_ref, wkv_ref, o0, o1, o2, stage_ref):
    tm = x_ref.shape[0]
    s_len = o0.shape[0]
    c = pl.program_id(1)
    gw = GROUP_WIDTH
    xhat = _rms_scale(x_ref[...])
    q = jnp.dot((xhat * gq_ref[...]).astype(BF16), wq_ref[...], preferred_element_type=F32)
    kv = jnp.dot((xhat * gkv_ref[...]).astype(BF16), wkv_ref[...], preferred_element_type=F32)
    for g, ((_, d), o_ref) in enumerate(zip(DIL_GROUPS, (o0, o1, o2))):
        qkv = jnp.concatenate([q[:, g * gw:(g + 1) * gw], kv[:, g * gw:(g + 1) * gw],
                               kv[:, DIL_WIDTH + g * gw:DIL_WIDTH + (g + 1) * gw]], axis=1)
        if d == 1:
            rows = pl.ds(pl.multiple_of(c * tm, tm), tm)
            o_ref[rows, :3 * gw] = qkv.astype(BF16)
            o_ref[rows, 3 * gw:] = q[:, DIL_WIDTH:].astype(BF16)
        else:
            n_cols = 3 * gw // LANES
            for j in range(n_cols):
                stage_ref[j] = qkv[:, j * LANES:(j + 1) * LANES]
            n = tm // d
            for r in range(d):
                dst = pl.ds(pl.multiple_of(r * (s_len // d) + c * n, n), n)
                o_ref[dst, :] = jnp.concatenate(
                    [stage_ref[j, pl.ds(r, n, stride=d), :] for j in range(n_cols)], axis=1).astype(BF16)


def _proj_b(h, g_kv, g_q, w_kv, w_in, batch, s_len):
    d = h.shape[1]
    tm = PROJ_ROWS
    n_tiles = s_len // tm
    widths = [3 * GROUP_WIDTH + MEM_WIDTH, 3 * GROUP_WIDTH, 3 * GROUP_WIDTH]
    return pl.pallas_call(
        _proj_b_kernel,
        grid=(batch, n_tiles),
        in_specs=[pl.BlockSpec((tm, d), lambda b, c: (b * n_tiles + c, 0)), _resident((1, d)), _resident((1, d)),
                  _resident(w_in.shape), _resident(w_kv.shape)],
        out_specs=[pl.BlockSpec((s_len, n), lambda b, c: (b, 0)) for n in widths],
        out_shape=[jax.ShapeDtypeStruct((batch * s_len, n), BF16) for n in widths],
        scratch_shapes=[pltpu.VMEM((3 * GROUP_WIDTH // LANES, tm, LANES), F32)],
        compiler_params=_params(("parallel", "arbitrary")),
        name="proj_b",
    )(h, g_kv.reshape(1, d), g_q.reshape(1, d), w_in.astype(BF16), w_kv.astype(BF16))


COMBINE_ROWS = 256


DIL_BLOCKS_PER_ITER = 4


def _softmax_parts(parts):
    m = parts[0]
    for t in parts[1:]:
        m = jnp.maximum(m, t)
    m = jnp.max(m, axis=-1, keepdims=True)
    return [jnp.exp(t - m) for t in parts], m


def _cross_attn_kernel(qkv0, qkv1, qkv2, kvm_ref, bias_first_ref, bias_ref, y0, y1, y2, om_ref,
                       o_cls, lse_cls, stage):
    s_len = qkv0.shape[0]
    head0 = _head0_lanes()
    gw = GROUP_WIDTH

    def block(g, qkv, p, row0, first):
        qc = slice(p * LANES, (p + 1) * LANES)
        kc = slice(gw + p * LANES, gw + (p + 1) * LANES)
        vc = slice(2 * gw + p * LANES, 2 * gw + (p + 1) * LANES)
        cur = pl.ds(pl.multiple_of(row0, BLK), BLK)
        q = qkv[cur, qc] * QK_SCALE
        kst = _split_heads(qkv[cur, kc], head0)
        vst = _with_den_cols(_split_heads(qkv[cur, vc], head0))
        if first:
            z = lax.dot_general(q, kst, NT_DIMS, preferred_element_type=F32) + bias_first_ref[g, p]
            heads = [[z[:, h * BLK:(h + 1) * BLK]] for h in range(2)]
        else:
            prev = pl.ds(pl.multiple_of(row0 - BLK, BLK), BLK)
            kst = jnp.concatenate([_split_heads(qkv[prev, kc], head0), kst], axis=0)
            vst = jnp.concatenate([_with_den_cols(_split_heads(qkv[prev, vc], head0)), vst], axis=0)
            z = lax.dot_general(q, kst, NT_DIMS, preferred_element_type=F32) + bias_ref[g, p]
            heads = [[z[:, h * BLK:(h + 1) * BLK], z[:, (2 + h) * BLK:(3 + h) * BLK]] for h in range(2)]
        (e0, m0), (e1, m1) = [_softmax_parts(parts) for parts in heads]
        pmat = jnp.concatenate([x for pair in zip(e0, e1) for x in pair], axis=1).astype(BF16)
        od = jnp.dot(pmat, vst, preferred_element_type=F32)
        o_cls[g, cur, qc] = od[:, :LANES] / od[:, LANES:]
        lse_cls[g, cur, qc] = jnp.where(head0, m0, m1) + jnp.log(od[:, LANES:])

    for g, ((_, d), qkv) in enumerate(zip(DIL_GROUPS, (qkv0, qkv1, qkv2))):
        blocks_per_class = s_len // d // BLK
        n_iters = s_len // BLK // DIL_BLOCKS_PER_ITER

        def blocks(it, starts_class, g=g, qkv=qkv, blocks_per_class=blocks_per_class):
            for k in range(DIL_BLOCKS_PER_ITER):
                first = blocks_per_class == 1 or (k == 0 and starts_class)
                for p in range(gw // LANES):
                    block(g, qkv, p, (it * DIL_BLOCKS_PER_ITER + k) * BLK, first)

        if blocks_per_class > DIL_BLOCKS_PER_ITER:
            blocks(0, True)
            lax.fori_loop(1, n_iters, lambda it, x, blocks=blocks: (blocks(it, False), x)[1], 0)
        else:
            lax.fori_loop(0, n_iters, lambda it, x, blocks=blocks: (blocks(it, True), x)[1], 0)

    def natural_rows(src, g, c, k):
        d = DIL_GROUPS[g][1]
        if d == 1:
            return src[g, pl.ds(pl.multiple_of(c * COMBINE_ROWS, COMBINE_ROWS), COMBINE_ROWS), :]
        n = COMBINE_ROWS // d
        n_cols = gw // LANES
        for r in range(d):
            rows = pl.ds(pl.multiple_of(r * (s_len // d) + c * n, n), n)
            for j in range(n_cols):
                stage[k * n_cols + j, pl.ds(r, n, stride=d), :] = src[g, rows, j * LANES:(j + 1) * LANES]
        return jnp.concatenate([stage[k * n_cols + j] for j in range(n_cols)], axis=1)

    def combine(c, carry):
        ls = [natural_rows(lse_cls, g, c, g) for g in range(3)]
        m = jnp.maximum(jnp.maximum(ls[0], ls[1]), ls[2])
        es = [jnp.exp(l - m) for l in ls]
        tot = es[0] + es[1] + es[2]
        rows = pl.ds(pl.multiple_of(c * COMBINE_ROWS, COMBINE_ROWS), COMBINE_ROWS)
        for g, y_ref in enumerate((y0, y1, y2)):
            y_ref[rows, :] = (natural_rows(o_cls, g, c, 3 + g) * (es[g] / tot)).astype(y_ref.dtype)
        return carry

    lax.fori_loop(0, s_len // COMBINE_ROWS, combine, 0)
    _mem_heads(qkv0.at[:, 3 * gw:], kvm_ref, om_ref)


def _dil_biases(slopes):
    i = jnp.arange(BLK)[:, None]
    j = jnp.arange(2 * BLK)[None, :]
    delta = i + BLK - j
    valid = (delta >= 0) & (delta <= BLK)
    firsts, others = [], []
    for g, (_, d) in enumerate(DIL_GROUPS):
        dist = (delta * d).astype(F32)
        f_p, o_p = [], []
        for p in range(2):
            halves = [jnp.where(valid, -slopes[g * HEADS_PER_GROUP + 2 * p + h] * dist, -jnp.inf) for h in range(2)]
            f_p.append(jnp.concatenate([halves[0][:, BLK:], halves[1][:, BLK:]], axis=1))
            o_p.append(jnp.concatenate([halves[0][:, :BLK], halves[1][:, :BLK],
                                        halves[0][:, BLK:], halves[1][:, BLK:]], axis=1))
        firsts.append(jnp.stack(f_p))
        others.append(jnp.stack(o_p))
    return jnp.stack(firsts), jnp.stack(others)


def _cross_attn(qkvs, kv_mem, batch, s_len):
    mem_len = kv_mem.shape[0] // batch
    slopes = 2.0 ** (-ALIBI_MAX_BIAS * jnp.arange(1, N_DIL_HEADS + 1, dtype=F32) / N_DIL_HEADS)
    bias_first, bias = _dil_biases(slopes)
    seq = lambda n: pl.BlockSpec((s_len, n), lambda b: (b, 0))
    n_out = len(DIL_GROUPS) + 1
    return pl.pallas_call(
        _cross_attn_kernel,
        grid=(batch,),
        in_specs=[seq(a.shape[1]) for a in qkvs]
        + [pl.BlockSpec((mem_len, 2 * MEM_WIDTH), lambda b: (b, 0)), _resident(bias_first.shape), _resident(bias.shape)],
        out_specs=[seq(GROUP_WIDTH)] * n_out,
        out_shape=[jax.ShapeDtypeStruct((batch * s_len, GROUP_WIDTH), BF16)] * n_out,
        scratch_shapes=[pltpu.VMEM((len(DIL_GROUPS), s_len, GROUP_WIDTH), F32)] * 2
        + [pltpu.VMEM((2 * len(DIL_GROUPS) * GROUP_WIDTH // LANES, COMBINE_ROWS, LANES), F32)],
        compiler_params=_params(("parallel",)),
        name="cross_attn",
    )(*qkvs, kv_mem, bias_first, bias)


SUBLANES = 8
FFN_ROWS = 512
FFN_CHUNK = 256
FFN_VREG_ROWS = FFN_ROWS // SUBLANES
FFN_PITCH = FFN_VREG_ROWS + SUBLANES


def _out_ffn_kernel(n_attn, tiles_per_seq, final, h_ref, *refs):
    a_refs, wo_refs = refs[:n_attn], refs[n_attn:2 * n_attn]
    g_ref, wup_ref, wconv_ref, wdown_ref = refs[2 * n_attn:2 * n_attn + 4]
    rest = refs[2 * n_attn + 4:]
    if final:
        gf_ref, out_ref, stage_ref, xe_ref, act_ref, tail_ref = rest
    else:
        out_ref, stage_ref, xe_ref, act_ref, tail_ref = rest
    n_cols = h_ref.shape[1] // LANES
    nv = FFN_VREG_ROWS
    i = pl.program_id(0)

    @pl.when(i == 0)
    def _():
        tail_ref[...] = jnp.zeros_like(tail_ref)

    h = h_ref[...]
    for a_ref, wo_ref in zip(a_refs, wo_refs):
        h = h + jnp.dot(a_ref[...], wo_ref[...], preferred_element_type=F32)
    xn = _rms_scale(h) * g_ref[...]

    for c in range(n_cols):
        for s in range(SUBLANES):
            stage_ref[c, s * FFN_PITCH:s * FFN_PITCH + nv, :] = xn[s * nv:(s + 1) * nv, c * LANES:(c + 1) * LANES]
    for jj in range(nv * SUBLANES // BF16_ROWS):
        vregs = [jnp.concatenate([stage_ref[c, pl.ds(j, SUBLANES, stride=FFN_PITCH), :] for c in range(n_cols)], axis=1)
                 for j in range(jj * BF16_ROWS // SUBLANES, (jj + 1) * BF16_ROWS // SUBLANES)]
        xe_ref[jj * BF16_ROWS:(jj + 1) * BF16_ROWS, :] = jnp.concatenate(vregs, axis=0).astype(xe_ref.dtype)

    starts_seq = i % tiles_per_seq == 0
    first_sublane = lax.broadcasted_iota(jnp.int32, (SUBLANES, FFN_CHUNK), 0) == 0

    def conv(u, wc, tail):
        def before_first(prev_vreg, last_vreg):
            return jnp.where(first_sublane, pltpu.roll(prev_vreg, 1, axis=0), pltpu.roll(last_vreg, 1, axis=0))

        back1 = before_first(tail[SUBLANES:], u[-SUBLANES:])
        back2 = before_first(tail[:SUBLANES], u[-2 * SUBLANES:-SUBLANES])
        u1 = jnp.concatenate([back1, u[:-SUBLANES]], axis=0)
        u2 = jnp.concatenate([back2, back1, u[:-2 * SUBLANES]], axis=0)
        return wc[0:1] * u2 + wc[1:2] * u1 + wc[2:3] * u

    xe = xe_ref[...]
    for c in range(D_FF // FFN_CHUNK):
        convs = []
        for part in range(2):
            cols = slice(part * D_FF + c * FFN_CHUNK, part * D_FF + (c + 1) * FFN_CHUNK)
            u = jnp.dot(xe, wup_ref[:, cols], preferred_element_type=F32)
            tail = tail_ref[2 * c + part]
            tail_ref[2 * c + part] = u[-2 * SUBLANES:]
            convs.append(conv(u, wconv_ref[:, cols], jnp.where(starts_seq, jnp.zeros_like(tail), tail)))
        half = 0.5 * convs[1]
        act_ref[:, c * FFN_CHUNK:(c + 1) * FFN_CHUNK] = ((half + half * jnp.tanh(half)) * convs[0]).astype(act_ref.dtype)

    y = jnp.dot(act_ref[...], wdown_ref[...], preferred_element_type=F32)
    for c in range(n_cols):
        for j in range(nv):
            stage_ref[c, pl.ds(j, SUBLANES, stride=FFN_PITCH), :] = y[j * SUBLANES:(j + 1) * SUBLANES,
                                                                      c * LANES:(c + 1) * LANES]
    y = h + jnp.concatenate(
        [jnp.concatenate([stage_ref[c, s * FFN_PITCH:s * FFN_PITCH + nv, :] for c in range(n_cols)], axis=1)
         for s in range(SUBLANES)], axis=0)
    if final:
        y = _rms_scale(y) * gf_ref[...]
    out_ref[...] = y


def _out_ffn(h, attn_parts, w_out, g_ffn, w_up, w_conv, w_down, s_len, g_final=None):
    t, d = h.shape
    tm = FFN_ROWS
    n_attn = len(attn_parts)
    final = g_final is not None
    offs = [0]
    for a in attn_parts:
        offs.append(offs[-1] + a.shape[1])
    wo_parts = [w_out[offs[k]:offs[k + 1]].astype(BF16) for k in range(n_attn)]
    row = lambda n: pl.BlockSpec((tm, n), lambda i: (i, 0))
    in_specs = [row(d)] + [row(a.shape[1]) for a in attn_parts] + [_resident(w.shape) for w in wo_parts]
    in_specs += [_resident((1, d)), _resident(w_up.shape), _resident(w_conv.shape), _resident(w_down.shape)]
    args = [h, *attn_parts, *wo_parts, g_ffn.reshape(1, d), w_up.astype(BF16), w_conv, w_down.astype(BF16)]
    if final:
        in_specs.append(_resident((1, d)))
        args.append(g_final.reshape(1, d))
    return pl.pallas_call(
        functools.partial(_out_ffn_kernel, n_attn, s_len // tm, final),
        grid=(t // tm,),
        in_specs=in_specs,
        out_specs=row(d),
        out_shape=jax.ShapeDtypeStruct((t, d), F32),
        scratch_shapes=[pltpu.VMEM((d // LANES, SUBLANES * FFN_PITCH, LANES), F32), pltpu.VMEM((tm, d), BF16),
                        pltpu.VMEM((tm, D_FF), BF16),
                        pltpu.VMEM((2 * D_FF // FFN_CHUNK, 2 * SUBLANES, FFN_CHUNK), F32)],
        compiler_params=_params(("arbitrary",)),
        name="out_ffn",
    )(*args)


def kernel(x, mem, a_norm_attn, a_w_in, a_w_out, a_norm_mem, a_w_mem_kv, a_norm_ffn, a_ffn_up, a_ffn_conv, a_ffn_down, kv_norm, w_kv_shared, b_norm_attn, b_w_in, b_w_out, b_norm_mem, b_w_mem_kv, b_norm_ffn, b_ffn_up, b_ffn_conv, b_ffn_down, final_norm):
    batch, s_len, d = x.shape
    assert a_w_in.shape[0] == 1 and b_w_in.shape[0] == 1, "one self-decoder and one cross-decoder layer"
    assert d == D_MODEL and s_len % (BLK * DIL_GROUPS[-1][1]) == 0 and s_len % FFN_ROWS == 0
    t = batch * s_len
    h = x.reshape(t, d)
    mem2 = mem.reshape(batch * mem.shape[1], d)

    kvm_a, kvm_b = _rms_proj(mem2, [a_norm_mem[0], b_norm_mem[0]],
                             [a_w_mem_kv[0].astype(BF16), b_w_mem_kv[0].astype(BF16)], PROJ_ROWS)

    (proj_a,) = _rms_proj(h, [a_norm_attn[0]], [a_w_in[0].astype(BF16)], PROJ_ROWS)
    o_sb = _sb_attn(proj_a, batch, s_len)
    o_mem = _mem_attn(proj_a, 3 * SB_WIDTH // MEM_WIDTH, kvm_a, batch, s_len)
    h = _out_ffn(h, [o_sb, o_mem], a_w_out[0], a_norm_ffn[0], a_ffn_up[0], a_ffn_conv[0], a_ffn_down[0], s_len)

    qkvs = _proj_b(h, kv_norm, b_norm_attn[0], w_kv_shared, b_w_in[0], batch, s_len)
    attn_parts = _cross_attn(qkvs, kvm_b, batch, s_len)
    h = _out_ffn(h, attn_parts, b_w_out[0], b_norm_ffn[0], b_ffn_up[0], b_ffn_conv[0], b_ffn_down[0], s_len,
                 g_final=final_norm)
    return h.reshape(batch, s_len, d)
```

```python
import functools

import jax
import jax.numpy as jnp
from jax import lax
from jax.experimental import pallas as pl
from jax.experimental.pallas import tpu as pltpu

D_MODEL = 1024
HEAD_DIM = 64
N_SB_HEADS = 12
N_MEM_HEADS = 4
DIL_GROUPS = ((128, 1), (512, 4), (2048, 16))
HEADS_PER_GROUP = 4
N_DIL_HEADS = HEADS_PER_GROUP * len(DIL_GROUPS)
SB_WIDTH = N_SB_HEADS * HEAD_DIM
MEM_WIDTH = N_MEM_HEADS * HEAD_DIM
DIL_WIDTH = N_DIL_HEADS * HEAD_DIM
GROUP_WIDTH = HEADS_PER_GROUP * HEAD_DIM
D_FF = 2816
CONV_WIDTH = 3
EPS = 1e-6
ALIBI_MAX_BIAS = 8.0
QK_SCALE = HEAD_DIM ** -0.5
LOG2E = 1.4426950408889634

LANES = 128
BF16_ROWS = 16
VMEM_LIMIT_BYTES = 56 * 1024 * 1024

BLK = 128
F32 = jnp.float32
BF16 = jnp.bfloat16
NT_DIMS = (((1,), (1,)), ((), ()))


def _params(semantics):
    return pltpu.CompilerParams(dimension_semantics=semantics, vmem_limit_bytes=VMEM_LIMIT_BYTES)


def _resident(shape):
    return pl.BlockSpec(shape, lambda *_: (0,) * len(shape), pipeline_mode=pl.Buffered(1))


def _head0_lanes():
    return lax.broadcasted_iota(jnp.int32, (1, LANES), 1) < HEAD_DIM


def _split_heads(t, head0):
    zero = jnp.zeros_like(t)
    return jnp.concatenate([jnp.where(head0, t, zero), jnp.where(head0, zero, t)], axis=0)


def _with_den_cols(vst):
    row = lax.broadcasted_iota(jnp.int32, vst.shape, 0)
    lane = lax.broadcasted_iota(jnp.int32, vst.shape, 1)
    owns = (row < vst.shape[0] // 2) == (lane < HEAD_DIM)
    return jnp.concatenate([vst, jnp.where(owns, 1.0, 0.0).astype(vst.dtype)], axis=1)


def _rms_scale(x):
    return x * lax.rsqrt(jnp.mean(x * x, axis=-1, keepdims=True) + EPS)


PROJ_ROWS = 512


def _rms_proj_kernel(n_out, x_ref, *refs):
    g_refs, w_refs, o_refs = refs[:n_out], refs[n_out:2 * n_out], refs[2 * n_out:]
    xhat = _rms_scale(x_ref[...])
    for g_ref, w_ref, o_ref in zip(g_refs, w_refs, o_refs):
        xn = (xhat * g_ref[...]).astype(BF16)
        o_ref[...] = jnp.dot(xn, w_ref[...], preferred_element_type=F32).astype(o_ref.dtype)


def _rms_proj(x, gains, weights, tm):
    t, d = x.shape
    n_out = len(gains)
    in_specs = [pl.BlockSpec((tm, d), lambda i: (i, 0))]
    in_specs += [_resident((1, d)) for _ in gains]
    in_specs += [_resident(w.shape) for w in weights]
    out_specs = [pl.BlockSpec((tm, w.shape[1]), lambda i: (i, 0)) for w in weights]
    out_shape = [jax.ShapeDtypeStruct((t, w.shape[1]), BF16) for w in weights]
    return pl.pallas_call(
        functools.partial(_rms_proj_kernel, n_out),
        grid=(t // tm,),
        in_specs=in_specs,
        out_specs=out_specs,
        out_shape=out_shape,
        compiler_params=_params(("parallel",)),
        name="rms_proj",
    )(x, *[g.reshape(1, d) for g in gains], *weights)


SB_QBLKS = 4
SB_ROWS = SB_QBLKS * BLK
SB_PAIRS = 6
SB_BAND = 3
SB_UNDERFLOW = 106.0


def _sb_kernel(q_ref, k_ref, v_ref, tri_ref, o_ref, carry_ref, acc_ref, lowest_ref):
    qt = pl.program_id(2)
    head0 = _head0_lanes()
    tri = tri_ref[...]
    pairs = range(SB_PAIRS)
    q_all = [q_ref[:, p * LANES:(p + 1) * LANES] * QK_SCALE for p in pairs]

    def key_block(p, k0, row_lo, row_hi, diagonal, carry, acc, live_from=None):
        cols = slice(p * LANES, (p + 1) * LANES)
        kst = _split_heads(k_ref[pl.ds(k0, BLK), cols], head0)
        vst = _split_heads(v_ref[pl.ds(k0, BLK), cols], head0)
        z = lax.dot_general(q_all[p][row_lo:row_hi], kst, NT_DIMS, preferred_element_type=F32)
        sp = jnp.maximum(z, 0.0) + jnp.log(1.0 + jnp.exp2(jnp.abs(z) * (-LOG2E)))
        if diagonal:
            t_rel = lax.broadcasted_iota(jnp.int32, (BLK, 2 * BLK), 0)
            s_rel = lax.broadcasted_iota(jnp.int32, (BLK, 2 * BLK), 1) & (BLK - 1)
            causal = s_rel < t_rel

            def mask(t):
                top = jnp.where(causal, t[:BLK], 0.0)
                return top if t.shape[0] == BLK else jnp.concatenate([top, t[BLK:]], axis=0)
        elif live_from is not None:
            live = lax.broadcasted_iota(jnp.int32, z.shape, 0) + row_lo >= live_from

            def mask(t):
                return jnp.where(live, t, 0.0)
        else:
            def mask(t):
                return t

        sp = mask(sp)
        sp16 = sp.astype(BF16)
        sums = [jnp.dot(sp16[:, h * BLK:(h + 1) * BLK], tri, preferred_element_type=F32)
                for h in range(2)]
        suffix = jnp.concatenate([s[:, :BLK] for s in sums], axis=1)
        total = jnp.concatenate([s[:, BLK:] for s in sums], axis=1)
        w = mask(jnp.exp2((z - (suffix + carry[row_lo:row_hi])) * LOG2E))
        pv = jnp.dot(w.astype(BF16), vst, preferred_element_type=F32)

        def all_rows(t):
            parts = [jnp.zeros((n, t.shape[1]), F32) if n else None for n in (row_lo, SB_ROWS - row_hi)]
            parts = [x for x in (parts[0], t, parts[1]) if x is not None]
            return t if len(parts) == 1 else jnp.concatenate(parts, axis=0)

        return carry + all_rows(total), acc + all_rows(pv)

    r0 = qt * SB_ROWS

    def band(lowest, width):
        carry = [jnp.zeros((SB_ROWS, 2 * BLK), F32) for _ in pairs]
        acc = [jnp.zeros((SB_ROWS, LANES), F32) for _ in pairs]
        for j in range(SB_QBLKS - 1, lowest - 1, -1):
            for p in pairs:
                carry[p], acc[p] = key_block(p, pl.multiple_of(r0 + j * BLK, BLK), max(j, 0) * BLK,
                                             min(j + width, SB_QBLKS) * BLK, j >= 0, carry[p], acc[p])
        lowest = None
        for p in pairs:
            carry_ref[p] = carry[p]
            acc_ref[p] = acc[p]
            o_ref[:, p * LANES:(p + 1) * LANES] = acc[p].astype(o_ref.dtype)
            lowest = jnp.min(carry[p]) if lowest is None else jnp.minimum(lowest, jnp.min(carry[p]))
        lowest_ref[0] = lowest

    @pl.when(qt == 0)
    def _():
        band(0, SB_QBLKS)

    @pl.when(qt > 0)
    def _():
        band(1 - SB_BAND, SB_BAND)

    top = SB_QBLKS - 1 - SB_BAND
    n_steps = top + 1 + qt * SB_QBLKS

    def step(state):
        j = top - state[0]
        lowest = None
        for p in pairs:
            carry, acc = key_block(p, pl.multiple_of(r0 + j * BLK, BLK), 0, SB_ROWS, False, carry_ref[p], acc_ref[p],
                                   live_from=(j + SB_BAND) * BLK)
            carry_ref[p] = carry
            acc_ref[p] = acc
            lowest = jnp.min(carry) if lowest is None else jnp.minimum(lowest, jnp.min(carry))
        return state[0] + 1, lowest

    @pl.when(jnp.logical_and(qt > 0, lowest_ref[0] < SB_UNDERFLOW))
    def _():
        lax.while_loop(lambda s: jnp.logical_and(s[0] < n_steps, s[1] < SB_UNDERFLOW), step,
                       (jnp.int32(0), lowest_ref[0]))
        for p in pairs:
            o_ref[:, p * LANES:(p + 1) * LANES] = acc_ref[p].astype(o_ref.dtype)


def _sb_tri_weights():
    j = jnp.arange(BLK)[:, None]
    c = jnp.arange(2 * BLK)[None, :]
    return ((c >= BLK) | (j >= c)).astype(BF16)


def _sb_attn(proj, batch, s_len):
    width = SB_PAIRS * LANES
    n_groups = SB_WIDTH // width
    n_qt = s_len // SB_ROWS
    kv_blk = (s_len, width)
    return pl.pallas_call(
        _sb_kernel,
        grid=(batch, n_groups, n_qt),
        in_specs=[
            pl.BlockSpec((SB_ROWS, width), lambda b, p, t: (b * n_qt + t, p)),
            pl.BlockSpec(kv_blk, lambda b, p, t: (b, n_groups + p)),
            pl.BlockSpec(kv_blk, lambda b, p, t: (b, 2 * n_groups + p)),
            _resident((BLK, 2 * BLK)),
        ],
        out_specs=pl.BlockSpec((SB_ROWS, width), lambda b, p, t: (b * n_qt + t, p)),
        out_shape=jax.ShapeDtypeStruct((batch * s_len, SB_WIDTH), BF16),
        scratch_shapes=[pltpu.VMEM((SB_PAIRS, SB_ROWS, 2 * BLK), F32), pltpu.VMEM((SB_PAIRS, SB_ROWS, LANES), F32),
                        pltpu.SMEM((1,), F32)],
        compiler_params=_params(("parallel", "parallel", "arbitrary")),
        name="sb_attn",
    )(proj, proj, proj, _sb_tri_weights())


MEM_ROWS = 512


def _mem_heads(q_ref, kv_ref, o_ref):
    s_len = q_ref.shape[0]
    mem_len = kv_ref.shape[0]
    head0 = _head0_lanes()
    n_pairs = MEM_WIDTH // LANES
    ksts = [_split_heads(kv_ref[:, p * LANES:(p + 1) * LANES], head0) for p in range(n_pairs)]
    vsts = [_with_den_cols(_split_heads(kv_ref[:, MEM_WIDTH + p * LANES:MEM_WIDTH + (p + 1) * LANES], head0))
            for p in range(n_pairs)]

    def step(c, carry):
        rows = pl.ds(pl.multiple_of(c * MEM_ROWS, MEM_ROWS), MEM_ROWS)
        for p in range(n_pairs):
            q = q_ref[rows, p * LANES:(p + 1) * LANES] * QK_SCALE
            z = lax.dot_general(q, ksts[p], NT_DIMS, preferred_element_type=F32)
            es = []
            for h in range(2):
                zh = z[:, h * mem_len:(h + 1) * mem_len]
                es.append(jnp.exp(zh - jnp.max(zh, axis=-1, keepdims=True)))
            od = jnp.dot(jnp.concatenate(es, axis=1).astype(BF16), vsts[p], preferred_element_type=F32)
            o_ref[rows, p * LANES:(p + 1) * LANES] = (od[:, :LANES] / od[:, LANES:]).astype(o_ref.dtype)
        return carry

    lax.fori_loop(0, s_len // MEM_ROWS, step, 0)


def _mem_attn(proj, q_col_block, kv_mem, batch, s_len):
    mem_len = kv_mem.shape[0] // batch
    return pl.pallas_call(
        _mem_heads,
        grid=(batch,),
        in_specs=[
            pl.BlockSpec((s_len, MEM_WIDTH), lambda b: (b, q_col_block)),
            pl.BlockSpec((mem_len, 2 * MEM_WIDTH), lambda b: (b, 0)),
        ],
        out_specs=pl.BlockSpec((s_len, MEM_WIDTH), lambda b: (b, 0)),
        out_shape=jax.ShapeDtypeStruct((batch * s_len, MEM_WIDTH), BF16),
        compiler_params=_params(("parallel",)),
        name="mem_attn",
    )(proj, kv_mem)


def _proj_b_kernel(x_ref, gkv_ref, gq_ref, wq_ref, wkv_ref, o0, o1, o2, stage_ref):
    tm = x_ref.shape[0]
    s_len = o0.shape[0]
    c = pl.program_id(1)
    gw = GROUP_WIDTH
    xhat = _rms_scale(x_ref[...])
    q = jnp.dot((xhat * gq_ref[...]).astype(BF16), wq_ref[...], preferred_element_type=F32)
    kv = jnp.dot((xhat * gkv_ref[...]).astype(BF16), wkv_ref[...], preferred_element_type=F32)
    for g, ((_, d), o_ref) in enumerate(zip(DIL_GROUPS, (o0, o1, o2))):
        qkv = jnp.concatenate([q[:, g * gw:(g + 1) * gw], kv[:, g * gw:(g + 1) * gw],
                               kv[:, DIL_WIDTH + g * gw:DIL_WIDTH + (g + 1) * gw]], axis=1)
        if d == 1:
            rows = pl.ds(pl.multiple_of(c * tm, tm), tm)
            o_ref[rows, :3 * gw] = qkv.astype(BF16)
            o_ref[rows, 3 * gw:] = q[:, DIL_WIDTH:].astype(BF16)
        else:
            n_cols = 3 * gw // LANES
            for j in range(n_cols):
                stage_ref[j] = qkv[:, j * LANES:(j + 1) * LANES]
            n = tm // d
            for r in range(d):
                dst = pl.ds(pl.multiple_of(r * (s_len // d) + c * n, n), n)
                o_ref[dst, :] = jnp.concatenate(
                    [stage_ref[j, pl.ds(r, n, stride=d), :] for j in range(n_cols)], axis=1).astype(BF16)


def _proj_b(h, g_kv, g_q, w_kv, w_in, batch, s_len):
    d = h.shape[1]
    tm = PROJ_ROWS
    n_tiles = s_len // tm
    widths = [3 * GROUP_WIDTH + MEM_WIDTH, 3 * GROUP_WIDTH, 3 * GROUP_WIDTH]
    return pl.pallas_call(
        _proj_b_kernel,
        grid=(batch, n_tiles),
        in_specs=[pl.BlockSpec((tm, d), lambda b, c: (b * n_tiles + c, 0)), _resident((1, d)), _resident((1, d)),
                  _resident(w_in.shape), _resident(w_kv.shape)],
        out_specs=[pl.BlockSpec((s_len, n), lambda b, c: (b, 0)) for n in widths],
        out_shape=[jax.ShapeDtypeStruct((batch * s_len, n), BF16) for n in widths],
        scratch_shapes=[pltpu.VMEM((3 * GROUP_WIDTH // LANES, tm, LANES), F32)],
        compiler_params=_params(("parallel", "arbitrary")),
        name="proj_b",
    )(h, g_kv.reshape(1, d), g_q.reshape(1, d), w_in.astype(BF16), w_kv.astype(BF16))


COMBINE_ROWS = 256


DIL_BLOCKS_PER_ITER = 4


def _softmax_parts(parts):
    m = parts[0]
    for t in parts[1:]:
        m = jnp.maximum(m, t)
    m = jnp.max(m, axis=-1, keepdims=True)
    return [jnp.exp(t - m) for t in parts], m


def _cross_attn_kernel(qkv0, qkv1, qkv2, kvm_ref, bias_first_ref, bias_ref, y0, y1, y2, om_ref,
                       o_cls, lse_cls, stage):
    s_len = qkv0.shape[0]
    head0 = _head0_lanes()
    gw = GROUP_WIDTH

    def block(g, qkv, p, row0, first):
        qc = slice(p * LANES, (p + 1) * LANES)
        kc = slice(gw + p * LANES, gw + (p + 1) * LANES)
        vc = slice(2 * gw + p * LANES, 2 * gw + (p + 1) * LANES)
        cur = pl.ds(pl.multiple_of(row0, BLK), BLK)
        q = qkv[cur, qc] * QK_SCALE
        kst = _split_heads(qkv[cur, kc], head0)
        vst = _with_den_cols(_split_heads(qkv[cur, vc], head0))
        if first:
            z = lax.dot_general(q, kst, NT_DIMS, preferred_element_type=F32) + bias_first_ref[g, p]
            heads = [[z[:, h * BLK:(h + 1) * BLK]] for h in range(2)]
        else:
            prev = pl.ds(pl.multiple_of(row0 - BLK, BLK), BLK)
            kst = jnp.concatenate([_split_heads(qkv[prev, kc], head0), kst], axis=0)
            vst = jnp.concatenate([_with_den_cols(_split_heads(qkv[prev, vc], head0)), vst], axis=0)
            z = lax.dot_general(q, kst, NT_DIMS, preferred_element_type=F32) + bias_ref[g, p]
            heads = [[z[:, h * BLK:(h + 1) * BLK], z[:, (2 + h) * BLK:(3 + h) * BLK]] for h in range(2)]
        (e0, m0), (e1, m1) = [_softmax_parts(parts) for parts in heads]
        pmat = jnp.concatenate([x for pair in zip(e0, e1) for x in pair], axis=1).astype(BF16)
        od = jnp.dot(pmat, vst, preferred_element_type=F32)
        o_cls[g, cur, qc] = od[:, :LANES] / od[:, LANES:]
        lse_cls[g, cur, qc] = jnp.where(head0, m0, m1) + jnp.log(od[:, LANES:])

    for g, ((_, d), qkv) in enumerate(zip(DIL_GROUPS, (qkv0, qkv1, qkv2))):
        blocks_per_class = s_len // d // BLK
        n_iters = s_len // BLK // DIL_BLOCKS_PER_ITER

        def blocks(it, starts_class, g=g, qkv=qkv, blocks_per_class=blocks_per_class):
            for k in range(DIL_BLOCKS_PER_ITER):
                first = blocks_per_class == 1 or (k == 0 and starts_class)
                for p in range(gw // LANES):
                    block(g, qkv, p, (it * DIL_BLOCKS_PER_ITER + k) * BLK, first)

        if blocks_per_class > DIL_BLOCKS_PER_ITER:
            blocks(0, True)
            lax.fori_loop(1, n_iters, lambda it, x, blocks=blocks: (blocks(it, False), x)[1], 0)
        else:
            lax.fori_loop(0, n_iters, lambda it, x, blocks=blocks: (blocks(it, True), x)[1], 0)

    def natural_rows(src, g, c, k):
        d = DIL_GROUPS[g][1]
        if d == 1:
            return src[g, pl.ds(pl.multiple_of(c * COMBINE_ROWS, COMBINE_ROWS), COMBINE_ROWS), :]
        n = COMBINE_ROWS // d
        n_cols = gw // LANES
        for r in range(d):
            rows = pl.ds(pl.multiple_of(r * (s_len // d) + c * n, n), n)
            for j in range(n_cols):
                stage[k * n_cols + j, pl.ds(r, n, stride=d), :] = src[g, rows, j * LANES:(j + 1) * LANES]
        return jnp.concatenate([stage[k * n_cols + j] for j in range(n_cols)], axis=1)

    def combine(c, carry):
        ls = [natural_rows(lse_cls, g, c, g) for g in range(3)]
        m = jnp.maximum(jnp.maximum(ls[0], ls[1]), ls[2])
        es = [jnp.exp(l - m) for l in ls]
        tot = es[0] + es[1] + es[2]
        rows = pl.ds(pl.multiple_of(c * COMBINE_ROWS, COMBINE_ROWS), COMBINE_ROWS)
        for g, y_ref in enumerate((y0, y1, y2)):
            y_ref[rows, :] = (natural_rows(o_cls, g, c, 3 + g) * (es[g] / tot)).astype(y_ref.dtype)
        return carry

    lax.fori_loop(0, s_len // COMBINE_ROWS, combine, 0)
    _mem_heads(qkv0.at[:, 3 * gw:], kvm_ref, om_ref)


def _dil_biases(slopes):
    i = jnp.arange(BLK)[:, None]
    j = jnp.arange(2 * BLK)[None, :]
    delta = i + BLK - j
    valid = (delta >= 0) & (delta <= BLK)
    firsts, others = [], []
    for g, (_, d) in enumerate(DIL_GROUPS):
        dist = (delta * d).astype(F32)
        f_p, o_p = [], []
        for p in range(2):
            halves = [jnp.where(valid, -slopes[g * HEADS_PER_GROUP + 2 * p + h] * dist, -jnp.inf) for h in range(2)]
            f_p.append(jnp.concatenate([halves[0][:, BLK:], halves[1][:, BLK:]], axis=1))
            o_p.append(jnp.concatenate([halves[0][:, :BLK], halves[1][:, :BLK],
                                        halves[0][:, BLK:], halves[1][:, BLK:]], axis=1))
        firsts.append(jnp.stack(f_p))
        others.append(jnp.stack(o_p))
    return jnp.stack(firsts), jnp.stack(others)


def _cross_attn(qkvs, kv_mem, batch, s_len):
    mem_len = kv_mem.shape[0] // batch
    slopes = 2.0 ** (-ALIBI_MAX_BIAS * jnp.arange(1, N_DIL_HEADS + 1, dtype=F32) / N_DIL_HEADS)
    bias_first, bias = _dil_biases(slopes)
    seq = lambda n: pl.BlockSpec((s_len, n), lambda b: (b, 0))
    n_out = len(DIL_GROUPS) + 1
    return pl.pallas_call(
        _cross_attn_kernel,
        grid=(batch,),
        in_specs=[seq(a.shape[1]) for a in qkvs]
        + [pl.BlockSpec((mem_len, 2 * MEM_WIDTH), lambda b: (b, 0)), _resident(bias_first.shape), _resident(bias.shape)],
        out_specs=[seq(GROUP_WIDTH)] * n_out,
        out_shape=[jax.ShapeDtypeStruct((batch * s_len, GROUP_WIDTH), BF16)] * n_out,
        scratch_shapes=[pltpu.VMEM((len(DIL_GROUPS), s_len, GROUP_WIDTH), F32)] * 2
        + [pltpu.VMEM((2 * len(DIL_GROUPS) * GROUP_WIDTH // LANES, COMBINE_ROWS, LANES), F32)],
        compiler_params=_params(("parallel",)),
        name="cross_attn",
    )(*qkvs, kv_mem, bias_first, bias)


SUBLANES = 8
FFN_ROWS = 512
FFN_CHUNK = 256
FFN_VREG_ROWS = FFN_ROWS // SUBLANES
FFN_PITCH = FFN_VREG_ROWS + SUBLANES


def _out_ffn_kernel(n_attn, tiles_per_seq, final, h_ref, *refs):
    a_refs, wo_refs = refs[:n_attn], refs[n_attn:2 * n_attn]
    g_ref, wup_ref, wconv_ref, wdown_ref = refs[2 * n_attn:2 * n_attn + 4]
    rest = refs[2 * n_attn + 4:]
    if final:
        gf_ref, out_ref, stage_ref, xe_ref, act_ref, tail_ref = rest
    else:
        out_ref, stage_ref, xe_ref, act_ref, tail_ref = rest
    n_cols = h_ref.shape[1] // LANES
    nv = FFN_VREG_ROWS
    i = pl.program_id(0)

    @pl.when(i == 0)
    def _():
        tail_ref[...] = jnp.zeros_like(tail_ref)

    h = h_ref[...]
    for a_ref, wo_ref in zip(a_refs, wo_refs):
        h = h + jnp.dot(a_ref[...], wo_ref[...], preferred_element_type=F32)
    xn = _rms_scale(h) * g_ref[...]

    for c in range(n_cols):
        for s in range(SUBLANES):
            stage_ref[c, s * FFN_PITCH:s * FFN_PITCH + nv, :] = xn[s * nv:(s + 1) * nv, c * LANES:(c + 1) * LANES]
    for jj in range(nv * SUBLANES // BF16_ROWS):
        vregs = [jnp.concatenate([stage_ref[c, pl.ds(j, SUBLANES, stride=FFN_PITCH), :] for c in range(n_cols)], axis=1)
                 for j in range(jj * BF16_ROWS // SUBLANES, (jj + 1) * BF16_ROWS // SUBLANES)]
        xe_ref[jj * BF16_ROWS:(jj + 1) * BF16_ROWS, :] = jnp.concatenate(vregs, axis=0).astype(xe_ref.dtype)

    starts_seq = i % tiles_per_seq == 0
    first_sublane = lax.broadcasted_iota(jnp.int32, (SUBLANES, FFN_CHUNK), 0) == 0

    def conv(u, wc, tail):
        def before_first(prev_vreg, last_vreg):
            return jnp.where(first_sublane, pltpu.roll(prev_vreg, 1, axis=0), pltpu.roll(last_vreg, 1, axis=0))

        back1 = before_first(tail[SUBLANES:], u[-SUBLANES:])
        back2 = before_first(tail[:SUBLANES], u[-2 * SUBLANES:-SUBLANES])
        u1 = jnp.concatenate([back1, u[:-SUBLANES]], axis=0)
        u2 = jnp.concatenate([back2, back1, u[:-2 * SUBLANES]], axis=0)
        return wc[0:1] * u2 + wc[1:2] * u1 + wc[2:3] * u

    xe = xe_ref[...]
    for c in range(D_FF // FFN_CHUNK):
        convs = []
        for part in range(2):
            cols = slice(part * D_FF + c * FFN_CHUNK, part * D_FF + (c + 1) * FFN_CHUNK)
            u = jnp.dot(xe, wup_ref[:, cols], preferred_element_type=F32)
            tail = tail_ref[2 * c + part]
            tail_ref[2 * c + part] = u[-2 * SUBLANES:]
            convs.append(conv(u, wconv_ref[:, cols], jnp.where(starts_seq, jnp.zeros_like(tail), tail)))
        half = 0.5 * convs[1]
        act_ref[:, c * FFN_CHUNK:(c + 1) * FFN_CHUNK] = ((half + half * jnp.tanh(half)) * convs[0]).astype(act_ref.dtype)

    y = jnp.dot(act_ref[...], wdown_ref[...], preferred_element_type=F32)
    for c in range(n_cols):
        for j in range(nv):
            stage_ref[c, pl.ds(j, SUBLANES, stride=FFN_PITCH), :] = y[j * SUBLANES:(j + 1) * SUBLANES,
                                                                      c * LANES:(c + 1) * LANES]
    y = h + jnp.concatenate(
        [jnp.concatenate([stage_ref[c, s * FFN_PITCH:s * FFN_PITCH + nv, :] for c in range(n_cols)], axis=1)
         for s in range(SUBLANES)], axis=0)
    if final:
        y = _rms_scale(y) * gf_ref[...]
    out_ref[...] = y


def _out_ffn(h, attn_parts, w_out, g_ffn, w_up, w_conv, w_down, s_len, g_final=None):
    t, d = h.shape
    tm = FFN_ROWS
    n_attn = len(attn_parts)
    final = g_final is not None
    offs = [0]
    for a in attn_parts:
        offs.append(offs[-1] + a.shape[1])
    wo_parts = [w_out[offs[k]:offs[k + 1]].astype(BF16) for k in range(n_attn)]
    row = lambda n: pl.BlockSpec((tm, n), lambda i: (i, 0))
    in_specs = [row(d)] + [row(a.shape[1]) for a in attn_parts] + [_resident(w.shape) for w in wo_parts]
    in_specs += [_resident((1, d)), _resident(w_up.shape), _resident(w_conv.shape), _resident(w_down.shape)]
    args = [h, *attn_parts, *wo_parts, g_ffn.reshape(1, d), w_up.astype(BF16), w_conv, w_down.astype(BF16)]
    if final:
        in_specs.append(_resident((1, d)))
        args.append(g_final.reshape(1, d))
    return pl.pallas_call(
        functools.partial(_out_ffn_kernel, n_attn, s_len // tm, final),
        grid=(t // tm,),
        in_specs=in_specs,
        out_specs=row(d),
        out_shape=jax.ShapeDtypeStruct((t, d), F32),
        scratch_shapes=[pltpu.VMEM((d // LANES, SUBLANES * FFN_PITCH, LANES), F32), pltpu.VMEM((tm, d), BF16),
                        pltpu.VMEM((tm, D_FF), BF16),
                        pltpu.VMEM((2 * D_FF // FFN_CHUNK, 2 * SUBLANES, FFN_CHUNK), F32)],
        compiler_params=_params(("arbitrary",)),
        name="out_ffn",
    )(*args)


def kernel(x, mem, a_norm_attn, a_w_in, a_w_out, a_norm_mem, a_w_mem_kv, a_norm_ffn, a_ffn_up, a_ffn_conv, a_ffn_down, kv_norm, w_kv_shared, b_norm_attn, b_w_in, b_w_out, b_norm_mem, b_w_mem_kv, b_norm_ffn, b_ffn_up, b_ffn_conv, b_ffn_down, final_norm):
    batch, s_len, d = x.shape
    assert a_w_in.shape[0] == 1 and b_w_in.shape[0] == 1, "one self-decoder and one cross-decoder layer"
    assert d == D_MODEL and s_len % (BLK * DIL_GROUPS[-1][1]) == 0 and s_len % FFN_ROWS == 0
    t = batch * s_len
    h = x.reshape(t, d)
    mem2 = mem.reshape(batch * mem.shape[1], d)

    kvm_a, kvm_b = _rms_proj(mem2, [a_norm_mem[0], b_norm_mem[0]],
                             [a_w_mem_kv[0].astype(BF16), b_w_mem_kv[0].astype(BF16)], PROJ_ROWS)

    (proj_a,) = _rms_proj(h, [a_norm_attn[0]], [a_w_in[0].astype(BF16)], PROJ_ROWS)
    o_sb = _sb_attn(proj_a, batch, s_len)
    o_mem = _mem_attn(proj_a, 3 * SB_WIDTH // MEM_WIDTH, kvm_a, batch, s_len)
    h = _out_ffn(h, [o_sb, o_mem], a_w_out[0], a_norm_ffn[0], a_ffn_up[0], a_ffn_conv[0], a_ffn_down[0], s_len)

    qkvs = _proj_b(h, kv_norm, b_norm_attn[0], w_kv_shared, b_w_in[0], batch, s_len)
    attn_parts = _cross_attn(qkvs, kvm_b, batch, s_len)
    h = _out_ffn(h, attn_parts, b_w_out[0], b_norm_ffn[0], b_ffn_up[0], b_ffn_conv[0], b_ffn_down[0], s_len,
                 g_final=final_norm)
    return h.reshape(batch, s_len, d)
```

```python
import functools

import jax
import jax.numpy as jnp
from jax import lax
from jax.experimental import pallas as pl
from jax.experimental.pallas import tpu as pltpu

D_MODEL = 1024
HEAD_DIM = 64
N_SB_HEADS = 12
N_MEM_HEADS = 4
DIL_GROUPS = ((128, 1), (512, 4), (2048, 16))
HEADS_PER_GROUP = 4
N_DIL_HEADS = HEADS_PER_GROUP * len(DIL_GROUPS)
SB_WIDTH = N_SB_HEADS * HEAD_DIM
MEM_WIDTH = N_MEM_HEADS * HEAD_DIM
DIL_WIDTH = N_DIL_HEADS * HEAD_DIM
GROUP_WIDTH = HEADS_PER_GROUP * HEAD_DIM
D_FF = 2816
CONV_WIDTH = 3
EPS = 1e-6
ALIBI_MAX_BIAS = 8.0
QK_SCALE = HEAD_DIM ** -0.5
LOG2E = 1.4426950408889634

LANES = 128
BF16_ROWS = 16
VMEM_LIMIT_BYTES = 56 * 1024 * 1024

BLK = 128
F32 = jnp.float32
BF16 = jnp.bfloat16
NT_DIMS = (((1,), (1,)), ((), ()))


def _params(semantics):
    return pltpu.CompilerParams(dimension_semantics=semantics, vmem_limit_bytes=VMEM_LIMIT_BYTES)


def _resident(shape):
    return pl.BlockSpec(shape, lambda *_: (0,) * len(shape), pipeline_mode=pl.Buffered(1))


def _head0_lanes():
    return lax.broadcasted_iota(jnp.int32, (1, LANES), 1) < HEAD_DIM


def _split_heads(t, head0):
    zero = jnp.zeros_like(t)
    return jnp.concatenate([jnp.where(head0, t, zero), jnp.where(head0, zero, t)], axis=0)


def _with_den_cols(vst):
    row = lax.broadcasted_iota(jnp.int32, vst.shape, 0)
    lane = lax.broadcasted_iota(jnp.int32, vst.shape, 1)
    owns = (row < vst.shape[0] // 2) == (lane < HEAD_DIM)
    return jnp.concatenate([vst, jnp.where(owns, 1.0, 0.0).astype(vst.dtype)], axis=1)


def _rms_scale(x):
    return x * lax.rsqrt(jnp.mean(x * x, axis=-1, keepdims=True) + EPS)


PROJ_ROWS = 512


def _rms_proj_kernel(n_out, x_ref, *refs):
    g_refs, w_refs, o_refs = refs[:n_out], refs[n_out:2 * n_out], refs[2 * n_out:]
    xhat = _rms_scale(x_ref[...])
    for g_ref, w_ref, o_ref in zip(g_refs, w_refs, o_refs):
        xn = (xhat * g_ref[...]).astype(BF16)
        o_ref[...] = jnp.dot(xn, w_ref[...], preferred_element_type=F32).astype(o_ref.dtype)


def _rms_proj(x, gains, weights, tm):
    t, d = x.shape
    n_out = len(gains)
    in_specs = [pl.BlockSpec((tm, d), lambda i: (i, 0))]
    in_specs += [_resident((1, d)) for _ in gains]
    in_specs += [_resident(w.shape) for w in weights]
    out_specs = [pl.BlockSpec((tm, w.shape[1]), lambda i: (i, 0)) for w in weights]
    out_shape = [jax.ShapeDtypeStruct((t, w.shape[1]), BF16) for w in weights]
    return pl.pallas_call(
        functools.partial(_rms_proj_kernel, n_out),
        grid=(t // tm,),
        in_specs=in_specs,
        out_specs=out_specs,
        out_shape=out_shape,
        compiler_params=_params(("parallel",)),
        name="rms_proj",
    )(x, *[g.reshape(1, d) for g in gains], *weights)


SB_QBLKS = 4
SB_ROWS = SB_QBLKS * BLK
SB_PAIRS = 6
SB_BAND = 3
SB_UNDERFLOW = 106.0


def _sb_kernel(q_ref, k_ref, v_ref, tri_ref, o_ref, carry_ref, acc_ref, lowest_ref):
    qt = pl.program_id(2)
    head0 = _head0_lanes()
    tri = tri_ref[...]
    pairs = range(SB_PAIRS)
    q_all = [q_ref[:, p * LANES:(p + 1) * LANES] * QK_SCALE for p in pairs]

    def key_block(p, k0, row_lo, row_hi, diagonal, carry, acc, live_from=None):
        cols = slice(p * LANES, (p + 1) * LANES)
        kst = _split_heads(k_ref[pl.ds(k0, BLK), cols], head0)
        vst = _split_heads(v_ref[pl.ds(k0, BLK), cols], head0)
        z = lax.dot_general(q_all[p][row_lo:row_hi], kst, NT_DIMS, preferred_element_type=F32)
        sp = jnp.maximum(z, 0.0) + jnp.log(1.0 + jnp.exp2(jnp.abs(z) * (-LOG2E)))
        if diagonal:
            t_rel = lax.broadcasted_iota(jnp.int32, (BLK, 2 * BLK), 0)
            s_rel = lax.broadcasted_iota(jnp.int32, (BLK, 2 * BLK), 1) & (BLK - 1)
            causal = s_rel < t_rel

            def mask(t):
                top = jnp.where(causal, t[:BLK], 0.0)
                return top if t.shape[0] == BLK else jnp.concatenate([top, t[BLK:]], axis=0)
        elif live_from is not None:
            live = lax.broadcasted_iota(jnp.int32, z.shape, 0) + row_lo >= live_from

            def mask(t):
                return jnp.where(live, t, 0.0)
        else:
            def mask(t):
                return t

        sp = mask(sp)
        sp16 = sp.astype(BF16)
        sums = [jnp.dot(sp16[:, h * BLK:(h + 1) * BLK], tri, preferred_element_type=F32)
                for h in range(2)]
        suffix = jnp.concatenate([s[:, :BLK] for s in sums], axis=1)
        total = jnp.concatenate([s[:, BLK:] for s in sums], axis=1)
        w = mask(jnp.exp2((z - (suffix + carry[row_lo:row_hi])) * LOG2E))
        pv = jnp.dot(w.astype(BF16), vst, preferred_element_type=F32)

        def all_rows(t):
            parts = [jnp.zeros((n, t.shape[1]), F32) if n else None for n in (row_lo, SB_ROWS - row_hi)]
            parts = [x for x in (parts[0], t, parts[1]) if x is not None]
            return t if len(parts) == 1 else jnp.concatenate(parts, axis=0)

        return carry + all_rows(total), acc + all_rows(pv)

    r0 = qt * SB_ROWS

    def band(lowest, width):
        carry = [jnp.zeros((SB_ROWS, 2 * BLK), F32) for _ in pairs]
        acc = [jnp.zeros((SB_ROWS, LANES), F32) for _ in pairs]
        for j in range(SB_QBLKS - 1, lowest - 1, -1):
            for p in pairs:
                carry[p], acc[p] = key_block(p, pl.multiple_of(r0 + j * BLK, BLK), max(j, 0) * BLK,
                                             min(j + width, SB_QBLKS) * BLK, j >= 0, carry[p], acc[p])
        lowest = None
        for p in pairs:
            carry_ref[p] = carry[p]
            acc_ref[p] = acc[p]
            o_ref[:, p * LANES:(p + 1) * LANES] = acc[p].astype(o_ref.dtype)
            lowest = jnp.min(carry[p]) if lowest is None else jnp.minimum(lowest, jnp.min(carry[p]))
        lowest_ref[0] = lowest

    @pl.when(qt == 0)
    def _():
        band(0, SB_QBLKS)

    @pl.when(qt > 0)
    def _():
        band(1 - SB_BAND, SB_BAND)

    top = SB_QBLKS - 1 - SB_BAND
    n_steps = top + 1 + qt * SB_QBLKS

    def step(state):
        j = top - state[0]
        lowest = None
        for p in pairs:
            carry, acc = key_block(p, pl.multiple_of(r0 + j * BLK, BLK), 0, SB_ROWS, False, carry_ref[p], acc_ref[p],
                                   live_from=(j + SB_BAND) * BLK)
            carry_ref[p] = carry
            acc_ref[p] = acc
            lowest = jnp.min(carry) if lowest is None else jnp.minimum(lowest, jnp.min(carry))
        return state[0] + 1, lowest

    @pl.when(jnp.logical_and(qt > 0, lowest_ref[0] < SB_UNDERFLOW))
    def _():
        lax.while_loop(lambda s: jnp.logical_and(s[0] < n_steps, s[1] < SB_UNDERFLOW), step,
                       (jnp.int32(0), lowest_ref[0]))
        for p in pairs:
            o_ref[:, p * LANES:(p + 1) * LANES] = acc_ref[p].astype(o_ref.dtype)


def _sb_tri_weights():
    j = jnp.arange(BLK)[:, None]
    c = jnp.arange(2 * BLK)[None, :]
    return ((c >= BLK) | (j >= c)).astype(BF16)


def _sb_attn(proj, batch, s_len):
    width = SB_PAIRS * LANES
    n_groups = SB_WIDTH // width
    n_qt = s_len // SB_ROWS
    kv_blk = (s_len, width)
    return pl.pallas_call(
        _sb_kernel,
        grid=(batch, n_groups, n_qt),
        in_specs=[
            pl.BlockSpec((SB_ROWS, width), lambda b, p, t: (b * n_qt + t, p)),
            pl.BlockSpec(kv_blk, lambda b, p, t: (b, n_groups + p)),
            pl.BlockSpec(kv_blk, lambda b, p, t: (b, 2 * n_groups + p)),
            _resident((BLK, 2 * BLK)),
        ],
        out_specs=pl.BlockSpec((SB_ROWS, width), lambda b, p, t: (b * n_qt + t, p)),
        out_shape=jax.ShapeDtypeStruct((batch * s_len, SB_WIDTH), BF16),
        scratch_shapes=[pltpu.VMEM((SB_PAIRS, SB_ROWS, 2 * BLK), F32), pltpu.VMEM((SB_PAIRS, SB_ROWS, LANES), F32),
                        pltpu.SMEM((1,), F32)],
        compiler_params=_params(("parallel", "parallel", "arbitrary")),
        name="sb_attn",
    )(proj, proj, proj, _sb_tri_weights())


MEM_ROWS = 1024


def _mem_heads(q_ref, kv_ref, o_ref):
    s_len = q_ref.shape[0]
    mem_len = kv_ref.shape[0]
    head0 = _head0_lanes()
    n_pairs = MEM_WIDTH // LANES
    ksts = [_split_heads(kv_ref[:, p * LANES:(p + 1) * LANES], head0) for p in range(n_pairs)]
    vsts = [_with_den_cols(_split_heads(kv_ref[:, MEM_WIDTH + p * LANES:MEM_WIDTH + (p + 1) * LANES], head0))
            for p in range(n_pairs)]

    def step(c, carry):
        rows = pl.ds(pl.multiple_of(c * MEM_ROWS, MEM_ROWS), MEM_ROWS)
        for p in range(n_pairs):
            q = q_ref[rows, p * LANES:(p + 1) * LANES] * QK_SCALE
            z = lax.dot_general(q, ksts[p], NT_DIMS, preferred_element_type=F32)
            es = []
            for h in range(2):
                zh = z[:, h * mem_len:(h + 1) * mem_len]
                es.append(jnp.exp(zh - jnp.max(zh, axis=-1, keepdims=True)))
            od = jnp.dot(jnp.concatenate(es, axis=1).astype(BF16), vsts[p], preferred_element_type=F32)
            o_ref[rows, p * LANES:(p + 1) * LANES] = (od[:, :LANES] / od[:, LANES:]).astype(o_ref.dtype)
        return carry

    lax.fori_loop(0, s_len // MEM_ROWS, step, 0)


def _mem_attn(proj, q_col_block, kv_mem, batch, s_len):
    mem_len = kv_mem.shape[0] // batch
    return pl.pallas_call(
        _mem_heads,
        grid=(batch,),
        in_specs=[
            pl.BlockSpec((s_len, MEM_WIDTH), lambda b: (b, q_col_block)),
            pl.BlockSpec((mem_len, 2 * MEM_WIDTH), lambda b: (b, 0)),
        ],
        out_specs=pl.BlockSpec((s_len, MEM_WIDTH), lambda b: (b, 0)),
        out_shape=jax.ShapeDtypeStruct((batch * s_len, MEM_WIDTH), BF16),
        compiler_params=_params(("parallel",)),
        name="mem_attn",
    )(proj, kv_mem)


def _proj_b_kernel(x_ref, gkv_ref, gq_ref, wq_ref, wkv_ref, o0, o1, o2, stage_ref):
    tm = x_ref.shape[0]
    s_len = o0.shape[0]
    c = pl.program_id(1)
    gw = GROUP_WIDTH
    xhat = _rms_scale(x_ref[...])
    q = jnp.dot((xhat * gq_ref[...]).astype(BF16), wq_ref[...], preferred_element_type=F32)
    kv = jnp.dot((xhat * gkv_ref[...]).astype(BF16), wkv_ref[...], preferred_element_type=F32)
    for g, ((_, d), o_ref) in enumerate(zip(DIL_GROUPS, (o0, o1, o2))):
        qkv = jnp.concatenate([q[:, g * gw:(g + 1) * gw], kv[:, g * gw:(g + 1) * gw],
                               kv[:, DIL_WIDTH + g * gw:DIL_WIDTH + (g + 1) * gw]], axis=1)
        if d == 1:
            rows = pl.ds(pl.multiple_of(c * tm, tm), tm)
            o_ref[rows, :3 * gw] = qkv.astype(BF16)
            o_ref[rows, 3 * gw:] = q[:, DIL_WIDTH:].astype(BF16)
        else:
            n_cols = 3 * gw // LANES
            for j in range(n_cols):
                stage_ref[j] = qkv[:, j * LANES:(j + 1) * LANES]
            n = tm // d
            for r in range(d):
                dst = pl.ds(pl.multiple_of(r * (s_len // d) + c * n, n), n)
                o_ref[dst, :] = jnp.concatenate(
                    [stage_ref[j, pl.ds(r, n, stride=d), :] for j in range(n_cols)], axis=1).astype(BF16)


def _proj_b(h, g_kv, g_q, w_kv, w_in, batch, s_len):
    d = h.shape[1]
    tm = PROJ_ROWS
    n_tiles = s_len // tm
    widths = [3 * GROUP_WIDTH + MEM_WIDTH, 3 * GROUP_WIDTH, 3 * GROUP_WIDTH]
    return pl.pallas_call(
        _proj_b_kernel,
        grid=(batch, n_tiles),
        in_specs=[pl.BlockSpec((tm, d), lambda b, c: (b * n_tiles + c, 0)), _resident((1, d)), _resident((1, d)),
                  _resident(w_in.shape), _resident(w_kv.shape)],
        out_specs=[pl.BlockSpec((s_len, n), lambda b, c: (b, 0)) for n in widths],
        out_shape=[jax.ShapeDtypeStruct((batch * s_len, n), BF16) for n in widths],
        scratch_shapes=[pltpu.VMEM((3 * GROUP_WIDTH // LANES, tm, LANES), F32)],
        compiler_params=_params(("parallel", "arbitrary")),
        name="proj_b",
    )(h, g_kv.reshape(1, d), g_q.reshape(1, d), w_in.astype(BF16), w_kv.astype(BF16))


COMBINE_ROWS = 256


DIL_BLOCKS_PER_ITER = 16


def _softmax_parts(parts):
    m = parts[0]
    for t in parts[1:]:
        m = jnp.maximum(m, t)
    m = jnp.max(m, axis=-1, keepdims=True)
    return [jnp.exp(t - m) for t in parts], m


def _cross_attn_kernel(qkv0, qkv1, qkv2, kvm_ref, bias_first_ref, bias_ref, y0, y1, y2, om_ref,
                       o_cls, lse_cls, stage):
    s_len = qkv0.shape[0]
    head0 = _head0_lanes()
    gw = GROUP_WIDTH

    def block(g, qkv, p, row0, first):
        qc = slice(p * LANES, (p + 1) * LANES)
        kc = slice(gw + p * LANES, gw + (p + 1) * LANES)
        vc = slice(2 * gw + p * LANES, 2 * gw + (p + 1) * LANES)
        cur = pl.ds(pl.multiple_of(row0, BLK), BLK)
        q = qkv[cur, qc] * QK_SCALE
        kst = _split_heads(qkv[cur, kc], head0)
        vst = _with_den_cols(_split_heads(qkv[cur, vc], head0))
        if first:
            z = lax.dot_general(q, kst, NT_DIMS, preferred_element_type=F32) + bias_first_ref[g, p]
            heads = [[z[:, h * BLK:(h + 1) * BLK]] for h in range(2)]
        else:
            prev = pl.ds(pl.multiple_of(row0 - BLK, BLK), BLK)
            kst = jnp.concatenate([_split_heads(qkv[prev, kc], head0), kst], axis=0)
            vst = jnp.concatenate([_with_den_cols(_split_heads(qkv[prev, vc], head0)), vst], axis=0)
            z = lax.dot_general(q, kst, NT_DIMS, preferred_element_type=F32) + bias_ref[g, p]
            heads = [[z[:, h * BLK:(h + 1) * BLK], z[:, (2 + h) * BLK:(3 + h) * BLK]] for h in range(2)]
        (e0, m0), (e1, m1) = [_softmax_parts(parts) for parts in heads]
        pmat = jnp.concatenate([x for pair in zip(e0, e1) for x in pair], axis=1).astype(BF16)
        od = jnp.dot(pmat, vst, preferred_element_type=F32)
        o_cls[g, cur, qc] = od[:, :LANES] / od[:, LANES:]
        lse_cls[g, cur, qc] = jnp.where(head0, m0, m1) + jnp.log(od[:, LANES:])

    for g, ((_, d), qkv) in enumerate(zip(DIL_GROUPS, (qkv0, qkv1, qkv2))):
        blocks_per_class = s_len // d // BLK
        n_iters = s_len // BLK // DIL_BLOCKS_PER_ITER

        def blocks(it, starts_class, g=g, qkv=qkv, blocks_per_class=blocks_per_class):
            for k in range(DIL_BLOCKS_PER_ITER):
                if blocks_per_class <= DIL_BLOCKS_PER_ITER:
                    first = k % blocks_per_class == 0
                else:
                    first = k == 0 and starts_class
                for p in range(gw // LANES):
                    block(g, qkv, p, (it * DIL_BLOCKS_PER_ITER + k) * BLK, first)

        blocks(0, True)
        if n_iters > 1:
            lax.fori_loop(1, n_iters, lambda it, x, blocks=blocks: (blocks(it, False), x)[1], 0)

    def natural_rows(src, g, c, k):
        d = DIL_GROUPS[g][1]
        if d == 1:
            return src[g, pl.ds(pl.multiple_of(c * COMBINE_ROWS, COMBINE_ROWS), COMBINE_ROWS), :]
        n = COMBINE_ROWS // d
        n_cols = gw // LANES
        for r in range(d):
            rows = pl.ds(pl.multiple_of(r * (s_len // d) + c * n, n), n)
            for j in range(n_cols):
                stage[k * n_cols + j, pl.ds(r, n, stride=d), :] = src[g, rows, j * LANES:(j + 1) * LANES]
        return jnp.concatenate([stage[k * n_cols + j] for j in range(n_cols)], axis=1)

    def combine(c, carry):
        ls = [natural_rows(lse_cls, g, c, g) for g in range(3)]
        m = jnp.maximum(jnp.maximum(ls[0], ls[1]), ls[2])
        es = [jnp.exp(l - m) for l in ls]
        tot = es[0] + es[1] + es[2]
        rows = pl.ds(pl.multiple_of(c * COMBINE_ROWS, COMBINE_ROWS), COMBINE_ROWS)
        for g, y_ref in enumerate((y0, y1, y2)):
            y_ref[rows, :] = (natural_rows(o_cls, g, c, 3 + g) * (es[g] / tot)).astype(y_ref.dtype)
        return carry

    lax.fori_loop(0, s_len // COMBINE_ROWS, combine, 0)
    _mem_heads(qkv0.at[:, 3 * gw:], kvm_ref, om_ref)


def _dil_biases(slopes):
    i = jnp.arange(BLK)[:, None]
    j = jnp.arange(2 * BLK)[None, :]
    delta = i + BLK - j
    valid = (delta >= 0) & (delta <= BLK)
    firsts, others = [], []
    for g, (_, d) in enumerate(DIL_GROUPS):
        dist = (delta * d).astype(F32)
        f_p, o_p = [], []
        for p in range(2):
            halves = [jnp.where(valid, -slopes[g * HEADS_PER_GROUP + 2 * p + h] * dist, -jnp.inf) for h in range(2)]
            f_p.append(jnp.concatenate([halves[0][:, BLK:], halves[1][:, BLK:]], axis=1))
            o_p.append(jnp.concatenate([halves[0][:, :BLK], halves[1][:, :BLK],
                                        halves[0][:, BLK:], halves[1][:, BLK:]], axis=1))
        firsts.append(jnp.stack(f_p))
        others.append(jnp.stack(o_p))
    return jnp.stack(firsts), jnp.stack(others)


def _cross_attn(qkvs, kv_mem, batch, s_len):
    mem_len = kv_mem.shape[0] // batch
    slopes = 2.0 ** (-ALIBI_MAX_BIAS * jnp.arange(1, N_DIL_HEADS + 1, dtype=F32) / N_DIL_HEADS)
    bias_first, bias = _dil_biases(slopes)
    seq = lambda n: pl.BlockSpec((s_len, n), lambda b: (b, 0))
    n_out = len(DIL_GROUPS) + 1
    return pl.pallas_call(
        _cross_attn_kernel,
        grid=(batch,),
        in_specs=[seq(a.shape[1]) for a in qkvs]
        + [pl.BlockSpec((mem_len, 2 * MEM_WIDTH), lambda b: (b, 0)), _resident(bias_first.shape), _resident(bias.shape)],
        out_specs=[seq(GROUP_WIDTH)] * n_out,
        out_shape=[jax.ShapeDtypeStruct((batch * s_len, GROUP_WIDTH), BF16)] * n_out,
        scratch_shapes=[pltpu.VMEM((len(DIL_GROUPS), s_len, GROUP_WIDTH), F32)] * 2
        + [pltpu.VMEM((2 * len(DIL_GROUPS) * GROUP_WIDTH // LANES, COMBINE_ROWS, LANES), F32)],
        compiler_params=_params(("parallel",)),
        name="cross_attn",
    )(*qkvs, kv_mem, bias_first, bias)


SUBLANES = 8
FFN_ROWS = 512
FFN_CHUNK = 256
FFN_VREG_ROWS = FFN_ROWS // SUBLANES
FFN_PITCH = FFN_VREG_ROWS + SUBLANES


def _out_ffn_kernel(n_attn, tiles_per_seq, final, h_ref, *refs):
    a_refs, wo_refs = refs[:n_attn], refs[n_attn:2 * n_attn]
    g_ref, wup_ref, wconv_ref, wdown_ref = refs[2 * n_attn:2 * n_attn + 4]
    rest = refs[2 * n_attn + 4:]
    if final:
        gf_ref, out_ref, stage_ref, xe_ref, act_ref, tail_ref = rest
    else:
        out_ref, stage_ref, xe_ref, act_ref, tail_ref = rest
    n_cols = h_ref.shape[1] // LANES
    nv = FFN_VREG_ROWS
    i = pl.program_id(0)

    @pl.when(i == 0)
    def _():
        tail_ref[...] = jnp.zeros_like(tail_ref)

    h = h_ref[...]
    for a_ref, wo_ref in zip(a_refs, wo_refs):
        h = h + jnp.dot(a_ref[...], wo_ref[...], preferred_element_type=F32)
    xn = _rms_scale(h) * g_ref[...]

    for c in range(n_cols):
        for s in range(SUBLANES):
            stage_ref[c, s * FFN_PITCH:s * FFN_PITCH + nv, :] = xn[s * nv:(s + 1) * nv, c * LANES:(c + 1) * LANES]
    for jj in range(nv * SUBLANES // BF16_ROWS):
        vregs = [jnp.concatenate([stage_ref[c, pl.ds(j, SUBLANES, stride=FFN_PITCH), :] for c in range(n_cols)], axis=1)
                 for j in range(jj * BF16_ROWS // SUBLANES, (jj + 1) * BF16_ROWS // SUBLANES)]
        xe_ref[jj * BF16_ROWS:(jj + 1) * BF16_ROWS, :] = jnp.concatenate(vregs, axis=0).astype(xe_ref.dtype)

    starts_seq = i % tiles_per_seq == 0
    first_sublane = lax.broadcasted_iota(jnp.int32, (SUBLANES, FFN_CHUNK), 0) == 0

    def conv(u, wc, tail):
        def before_first(prev_vreg, last_vreg):
            return jnp.where(first_sublane, pltpu.roll(prev_vreg, 1, axis=0), pltpu.roll(last_vreg, 1, axis=0))

        back1 = before_first(tail[SUBLANES:], u[-SUBLANES:])
        back2 = before_first(tail[:SUBLANES], u[-2 * SUBLANES:-SUBLANES])
        u1 = jnp.concatenate([back1, u[:-SUBLANES]], axis=0)
        u2 = jnp.concatenate([back2, back1, u[:-2 * SUBLANES]], axis=0)
        return wc[0:1] * u2 + wc[1:2] * u1 + wc[2:3] * u

    xe = xe_ref[...]
    for c in range(D_FF // FFN_CHUNK):
        convs = []
        for part in range(2):
            cols = slice(part * D_FF + c * FFN_CHUNK, part * D_FF + (c + 1) * FFN_CHUNK)
            u = jnp.dot(xe, wup_ref[:, cols], preferred_element_type=F32)
            tail = tail_ref[2 * c + part]
            tail_ref[2 * c + part] = u[-2 * SUBLANES:]
            convs.append(conv(u, wconv_ref[:, cols], jnp.where(starts_seq, jnp.zeros_like(tail), tail)))
        half = 0.5 * convs[1]
        act_ref[:, c * FFN_CHUNK:(c + 1) * FFN_CHUNK] = ((half + half * jnp.tanh(half)) * convs[0]).astype(act_ref.dtype)

    y = jnp.dot(act_ref[...], wdown_ref[...], preferred_element_type=F32)
    for c in range(n_cols):
        for j in range(nv):
            stage_ref[c, pl.ds(j, SUBLANES, stride=FFN_PITCH), :] = y[j * SUBLANES:(j + 1) * SUBLANES,
                                                                      c * LANES:(c + 1) * LANES]
    y = h + jnp.concatenate(
        [jnp.concatenate([stage_ref[c, s * FFN_PITCH:s * FFN_PITCH + nv, :] for c in range(n_cols)], axis=1)
         for s in range(SUBLANES)], axis=0)
    if final:
        y = _rms_scale(y) * gf_ref[...]
    out_ref[...] = y


def _out_ffn(h, attn_parts, w_out, g_ffn, w_up, w_conv, w_down, s_len, g_final=None):
    t, d = h.shape
    tm = FFN_ROWS
    n_attn = len(attn_parts)
    final = g_final is not None
    offs = [0]
    for a in attn_parts:
        offs.append(offs[-1] + a.shape[1])
    wo_parts = [w_out[offs[k]:offs[k + 1]].astype(BF16) for k in range(n_attn)]
    row = lambda n: pl.BlockSpec((tm, n), lambda i: (i, 0))
    in_specs = [row(d)] + [row(a.shape[1]) for a in attn_parts] + [_resident(w.shape) for w in wo_parts]
    in_specs += [_resident((1, d)), _resident(w_up.shape), _resident(w_conv.shape), _resident(w_down.shape)]
    args = [h, *attn_parts, *wo_parts, g_ffn.reshape(1, d), w_up.astype(BF16), w_conv, w_down.astype(BF16)]
    if final:
        in_specs.append(_resident((1, d)))
        args.append(g_final.reshape(1, d))
    return pl.pallas_call(
        functools.partial(_out_ffn_kernel, n_attn, s_len // tm, final),
        grid=(t // tm,),
        in_specs=in_specs,
        out_specs=row(d),
        out_shape=jax.ShapeDtypeStruct((t, d), F32),
        scratch_shapes=[pltpu.VMEM((d // LANES, SUBLANES * FFN_PITCH, LANES), F32), pltpu.VMEM((tm, d), BF16),
                        pltpu.VMEM((tm, D_FF), BF16),
                        pltpu.VMEM((2 * D_FF // FFN_CHUNK, 2 * SUBLANES, FFN_CHUNK), F32)],
        compiler_params=_params(("arbitrary",)),
        name="out_ffn",
    )(*args)


def kernel(x, mem, a_norm_attn, a_w_in, a_w_out, a_norm_mem, a_w_mem_kv, a_norm_ffn, a_ffn_up, a_ffn_conv, a_ffn_down, kv_norm, w_kv_shared, b_norm_attn, b_w_in, b_w_out, b_norm_mem, b_w_mem_kv, b_norm_ffn, b_ffn_up, b_ffn_conv, b_ffn_down, final_norm):
    batch, s_len, d = x.shape
    assert a_w_in.shape[0] == 1 and b_w_in.shape[0] == 1, "one self-decoder and one cross-decoder layer"
    assert d == D_MODEL and s_len % (BLK * DIL_GROUPS[-1][1]) == 0 and s_len % FFN_ROWS == 0
    t = batch * s_len
    h = x.reshape(t, d)
    mem2 = mem.reshape(batch * mem.shape[1], d)

    kvm_a, kvm_b = _rms_proj(mem2, [a_norm_mem[0], b_norm_mem[0]],
                             [a_w_mem_kv[0].astype(BF16), b_w_mem_kv[0].astype(BF16)], PROJ_ROWS)

    (proj_a,) = _rms_proj(h, [a_norm_attn[0]], [a_w_in[0].astype(BF16)], PROJ_ROWS)
    o_sb = _sb_attn(proj_a, batch, s_len)
    o_mem = _mem_attn(proj_a, 3 * SB_WIDTH // MEM_WIDTH, kvm_a, batch, s_len)
    h = _out_ffn(h, [o_sb, o_mem], a_w_out[0], a_norm_ffn[0], a_ffn_up[0], a_ffn_conv[0], a_ffn_down[0], s_len)

    qkvs = _proj_b(h, kv_norm, b_norm_attn[0], w_kv_shared, b_w_in[0], batch, s_len)
    attn_parts = _cross_attn(qkvs, kvm_b, batch, s_len)
    h = _out_ffn(h, attn_parts, b_w_out[0], b_norm_ffn[0], b_ffn_up[0], b_ffn_conv[0], b_ffn_down[0], s_len,
                 g_final=final_norm)
    return h.reshape(batch, s_len, d)
```

```python
import functools

import jax
import jax.numpy as jnp
from jax import lax
from jax.experimental import pallas as pl
from jax.experimental.pallas import tpu as pltpu

D_MODEL = 1024
HEAD_DIM = 64
N_SB_HEADS = 12
N_MEM_HEADS = 4
DIL_GROUPS = ((128, 1), (512, 4), (2048, 16))
HEADS_PER_GROUP = 4
N_DIL_HEADS = HEADS_PER_GROUP * len(DIL_GROUPS)
SB_WIDTH = N_SB_HEADS * HEAD_DIM
MEM_WIDTH = N_MEM_HEADS * HEAD_DIM
DIL_WIDTH = N_DIL_HEADS * HEAD_DIM
GROUP_WIDTH = HEADS_PER_GROUP * HEAD_DIM
D_FF = 2816
CONV_WIDTH = 3
EPS = 1e-6
ALIBI_MAX_BIAS = 8.0
QK_SCALE = HEAD_DIM ** -0.5
LOG2E = 1.4426950408889634

LANES = 128
BF16_ROWS = 16
VMEM_LIMIT_BYTES = 56 * 1024 * 1024

BLK = 128
F32 = jnp.float32
BF16 = jnp.bfloat16
NT_DIMS = (((1,), (1,)), ((), ()))


def _params(semantics):
    return pltpu.CompilerParams(dimension_semantics=semantics, vmem_limit_bytes=VMEM_LIMIT_BYTES)


def _resident(shape):
    return pl.BlockSpec(shape, lambda *_: (0,) * len(shape), pipeline_mode=pl.Buffered(1))


def _head0_lanes():
    return lax.broadcasted_iota(jnp.int32, (1, LANES), 1) < HEAD_DIM


def _split_heads(t, head0):
    zero = jnp.zeros_like(t)
    return jnp.concatenate([jnp.where(head0, t, zero), jnp.where(head0, zero, t)], axis=0)


def _with_den_cols(vst):
    row = lax.broadcasted_iota(jnp.int32, vst.shape, 0)
    lane = lax.broadcasted_iota(jnp.int32, vst.shape, 1)
    owns = (row < vst.shape[0] // 2) == (lane < HEAD_DIM)
    return jnp.concatenate([vst, jnp.where(owns, 1.0, 0.0).astype(vst.dtype)], axis=1)


def _rms_scale(x):
    return x * lax.rsqrt(jnp.mean(x * x, axis=-1, keepdims=True) + EPS)


PROJ_ROWS = 1024


def _rms_proj_kernel(n_out, x_ref, *refs):
    g_refs, w_refs, o_refs = refs[:n_out], refs[n_out:2 * n_out], refs[2 * n_out:]
    xhat = _rms_scale(x_ref[...])
    for g_ref, w_ref, o_ref in zip(g_refs, w_refs, o_refs):
        xn = (xhat * g_ref[...]).astype(BF16)
        o_ref[...] = jnp.dot(xn, w_ref[...], preferred_element_type=F32).astype(o_ref.dtype)


def _rms_proj(x, gains, weights, tm):
    t, d = x.shape
    n_out = len(gains)
    in_specs = [pl.BlockSpec((tm, d), lambda i: (i, 0))]
    in_specs += [_resident((1, d)) for _ in gains]
    in_specs += [_resident(w.shape) for w in weights]
    out_specs = [pl.BlockSpec((tm, w.shape[1]), lambda i: (i, 0)) for w in weights]
    out_shape = [jax.ShapeDtypeStruct((t, w.shape[1]), BF16) for w in weights]
    return pl.pallas_call(
        functools.partial(_rms_proj_kernel, n_out),
        grid=(t // tm,),
        in_specs=in_specs,
        out_specs=out_specs,
        out_shape=out_shape,
        compiler_params=_params(("parallel",)),
        name="rms_proj",
    )(x, *[g.reshape(1, d) for g in gains], *weights)


SB_QBLKS = 8
SB_ROWS = SB_QBLKS * BLK
SB_PAIRS = 6
SB_BAND = 3
SB_UNDERFLOW = 106.0
assert SB_BAND < SB_QBLKS


def _sb_kernel(q_ref, k_ref, v_ref, tri_ref, o_ref, carry_ref, acc_ref, lowest_ref):
    qt = pl.program_id(2)
    head0 = _head0_lanes()
    tri = tri_ref[...]
    pairs = range(SB_PAIRS)
    q_all = [q_ref[:, p * LANES:(p + 1) * LANES] * QK_SCALE for p in pairs]

    def key_block(p, k0, row_lo, row_hi, diagonal, carry, acc, live_from=None):
        cols = slice(p * LANES, (p + 1) * LANES)
        kst = _split_heads(k_ref[pl.ds(k0, BLK), cols], head0)
        vst = _split_heads(v_ref[pl.ds(k0, BLK), cols], head0)
        z = lax.dot_general(q_all[p][row_lo:row_hi], kst, NT_DIMS, preferred_element_type=F32)
        sp = jnp.maximum(z, 0.0) + jnp.log(1.0 + jnp.exp2(jnp.abs(z) * (-LOG2E)))
        if diagonal:
            t_rel = lax.broadcasted_iota(jnp.int32, (BLK, 2 * BLK), 0)
            s_rel = lax.broadcasted_iota(jnp.int32, (BLK, 2 * BLK), 1) & (BLK - 1)
            causal = s_rel < t_rel

            def mask(t):
                top = jnp.where(causal, t[:BLK], 0.0)
                return top if t.shape[0] == BLK else jnp.concatenate([top, t[BLK:]], axis=0)
        elif live_from is not None:
            live = lax.broadcasted_iota(jnp.int32, z.shape, 0) + row_lo >= live_from

            def mask(t):
                return jnp.where(live, t, 0.0)
        else:
            def mask(t):
                return t

        sp = mask(sp)
        sp16 = sp.astype(BF16)
        sums = [jnp.dot(sp16[:, h * BLK:(h + 1) * BLK], tri, preferred_element_type=F32)
                for h in range(2)]
        suffix = jnp.concatenate([s[:, :BLK] for s in sums], axis=1)
        total = jnp.concatenate([s[:, BLK:] for s in sums], axis=1)
        w = mask(jnp.exp2((z - (suffix + carry[row_lo:row_hi])) * LOG2E))
        pv = jnp.dot(w.astype(BF16), vst, preferred_element_type=F32)

        def all_rows(t):
            parts = [jnp.zeros((n, t.shape[1]), F32) if n else None for n in (row_lo, SB_ROWS - row_hi)]
            parts = [x for x in (parts[0], t, parts[1]) if x is not None]
            return t if len(parts) == 1 else jnp.concatenate(parts, axis=0)

        return carry + all_rows(total), acc + all_rows(pv)

    r0 = qt * SB_ROWS

    def band(lowest, pending_from):
        carry = [jnp.zeros((SB_ROWS, 2 * BLK), F32) for _ in pairs]
        acc = [jnp.zeros((SB_ROWS, LANES), F32) for _ in pairs]
        for j in range(SB_QBLKS - 1, lowest - 1, -1):
            for p in pairs:
                carry[p], acc[p] = key_block(p, pl.multiple_of(r0 + j * BLK, BLK), max(j, 0) * BLK,
                                             min(j + SB_BAND, SB_QBLKS) * BLK, j >= 0, carry[p], acc[p])
        smallest = None
        for p in pairs:
            carry_ref[p] = carry[p]
            acc_ref[p] = acc[p]
            o_ref[:, p * LANES:(p + 1) * LANES] = acc[p].astype(o_ref.dtype)
            m = jnp.min(carry[p][pending_from:])
            smallest = m if smallest is None else jnp.minimum(smallest, m)
        lowest_ref[0] = smallest

    @pl.when(qt == 0)
    def _():
        band(0, SB_BAND * BLK)

    @pl.when(qt > 0)
    def _():
        band(1 - SB_BAND, 0)

    top = SB_QBLKS - 1 - SB_BAND
    n_steps = top + 1 + qt * SB_QBLKS

    def step(state):
        j = top - state[0]
        lowest = None
        for p in pairs:
            carry, acc = key_block(p, pl.multiple_of(r0 + j * BLK, BLK), 0, SB_ROWS, False, carry_ref[p], acc_ref[p],
                                   live_from=(j + SB_BAND) * BLK)
            carry_ref[p] = carry
            acc_ref[p] = acc
            lowest = jnp.min(carry) if lowest is None else jnp.minimum(lowest, jnp.min(carry))
        return state[0] + 1, lowest

    @pl.when(lowest_ref[0] < SB_UNDERFLOW)
    def _():
        lax.while_loop(lambda s: jnp.logical_and(s[0] < n_steps, s[1] < SB_UNDERFLOW), step,
                       (jnp.int32(0), lowest_ref[0]))
        for p in pairs:
            o_ref[:, p * LANES:(p + 1) * LANES] = acc_ref[p].astype(o_ref.dtype)


def _sb_tri_weights():
    j = jnp.arange(BLK)[:, None]
    c = jnp.arange(2 * BLK)[None, :]
    return ((c >= BLK) | (j >= c)).astype(BF16)


def _sb_attn(proj, batch, s_len):
    width = SB_PAIRS * LANES
    n_groups = SB_WIDTH // width
    n_qt = s_len // SB_ROWS
    kv_blk = (s_len, width)
    return pl.pallas_call(
        _sb_kernel,
        grid=(batch, n_groups, n_qt),
        in_specs=[
            pl.BlockSpec((SB_ROWS, width), lambda b, p, t: (b * n_qt + t, p)),
            pl.BlockSpec(kv_blk, lambda b, p, t: (b, n_groups + p)),
            pl.BlockSpec(kv_blk, lambda b, p, t: (b, 2 * n_groups + p)),
            _resident((BLK, 2 * BLK)),
        ],
        out_specs=pl.BlockSpec((SB_ROWS, width), lambda b, p, t: (b * n_qt + t, p)),
        out_shape=jax.ShapeDtypeStruct((batch * s_len, SB_WIDTH), BF16),
        scratch_shapes=[pltpu.VMEM((SB_PAIRS, SB_ROWS, 2 * BLK), F32), pltpu.VMEM((SB_PAIRS, SB_ROWS, LANES), F32),
                        pltpu.SMEM((1,), F32)],
        compiler_params=_params(("parallel", "parallel", "arbitrary")),
        name="sb_attn",
    )(proj, proj, proj, _sb_tri_weights())


MEM_ROWS = 1024


def _mem_heads(q_ref, kv_ref, o_ref):
    s_len = q_ref.shape[0]
    mem_len = kv_ref.shape[0]
    head0 = _head0_lanes()
    n_pairs = MEM_WIDTH // LANES
    ksts = [_split_heads(kv_ref[:, p * LANES:(p + 1) * LANES], head0) for p in range(n_pairs)]
    vsts = [_with_den_cols(_split_heads(kv_ref[:, MEM_WIDTH + p * LANES:MEM_WIDTH + (p + 1) * LANES], head0))
            for p in range(n_pairs)]

    def step(c, carry):
        rows = pl.ds(pl.multiple_of(c * MEM_ROWS, MEM_ROWS), MEM_ROWS)
        for p in range(n_pairs):
            q = q_ref[rows, p * LANES:(p + 1) * LANES] * QK_SCALE
            z = lax.dot_general(q, ksts[p], NT_DIMS, preferred_element_type=F32)
            es = []
            for h in range(2):
                zh = z[:, h * mem_len:(h + 1) * mem_len]
                es.append(jnp.exp(zh - jnp.max(zh, axis=-1, keepdims=True)))
            od = jnp.dot(jnp.concatenate(es, axis=1).astype(BF16), vsts[p], preferred_element_type=F32)
            o_ref[rows, p * LANES:(p + 1) * LANES] = (od[:, :LANES] / od[:, LANES:]).astype(o_ref.dtype)
        return carry

    lax.fori_loop(0, s_len // MEM_ROWS, step, 0)


def _mem_attn(proj, q_col_block, kv_mem, batch, s_len):
    mem_len = kv_mem.shape[0] // batch
    return pl.pallas_call(
        _mem_heads,
        grid=(batch,),
        in_specs=[
            pl.BlockSpec((s_len, MEM_WIDTH), lambda b: (b, q_col_block)),
            pl.BlockSpec((mem_len, 2 * MEM_WIDTH), lambda b: (b, 0)),
        ],
        out_specs=pl.BlockSpec((s_len, MEM_WIDTH), lambda b: (b, 0)),
        out_shape=jax.ShapeDtypeStruct((batch * s_len, MEM_WIDTH), BF16),
        compiler_params=_params(("parallel",)),
        name="mem_attn",
    )(proj, kv_mem)


def _proj_b_kernel(x_ref, gkv_ref, gq_ref, wq_ref, wkv_ref, o0, o1, o2, stage_ref):
    tm = x_ref.shape[0]
    s_len = o0.shape[0]
    c = pl.program_id(1)
    gw = GROUP_WIDTH
    xhat = _rms_scale(x_ref[...])
    q = jnp.dot((xhat * gq_ref[...]).astype(BF16), wq_ref[...], preferred_element_type=F32)
    kv = jnp.dot((xhat * gkv_ref[...]).astype(BF16), wkv_ref[...], preferred_element_type=F32)
    for g, ((_, d), o_ref) in enumerate(zip(DIL_GROUPS, (o0, o1, o2))):
        qkv = jnp.concatenate([q[:, g * gw:(g + 1) * gw], kv[:, g * gw:(g + 1) * gw],
                               kv[:, DIL_WIDTH + g * gw:DIL_WIDTH + (g + 1) * gw]], axis=1)
        if d == 1:
            rows = pl.ds(pl.multiple_of(c * tm, tm), tm)
            o_ref[rows, :3 * gw] = qkv.astype(BF16)
            o_ref[rows, 3 * gw:] = q[:, DIL_WIDTH:].astype(BF16)
        else:
            n_cols = 3 * gw // LANES
            for j in range(n_cols):
                stage_ref[j] = qkv[:, j * LANES:(j + 1) * LANES]
            n = tm // d
            for r in range(d):
                dst = pl.ds(pl.multiple_of(r * (s_len // d) + c * n, n), n)
                o_ref[dst, :] = jnp.concatenate(
                    [stage_ref[j, pl.ds(r, n, stride=d), :] for j in range(n_cols)], axis=1).astype(BF16)


def _proj_b(h, g_kv, g_q, w_kv, w_in, batch, s_len):
    d = h.shape[1]
    tm = PROJ_ROWS
    n_tiles = s_len // tm
    widths = [3 * GROUP_WIDTH + MEM_WIDTH, 3 * GROUP_WIDTH, 3 * GROUP_WIDTH]
    return pl.pallas_call(
        _proj_b_kernel,
        grid=(batch, n_tiles),
        in_specs=[pl.BlockSpec((tm, d), lambda b, c: (b * n_tiles + c, 0)), _resident((1, d)), _resident((1, d)),
                  _resident(w_in.shape), _resident(w_kv.shape)],
        out_specs=[pl.BlockSpec((s_len, n), lambda b, c: (b, 0)) for n in widths],
        out_shape=[jax.ShapeDtypeStruct((batch * s_len, n), BF16) for n in widths],
        scratch_shapes=[pltpu.VMEM((3 * GROUP_WIDTH // LANES, tm, LANES), F32)],
        compiler_params=_params(("parallel", "arbitrary")),
        name="proj_b",
    )(h, g_kv.reshape(1, d), g_q.reshape(1, d), w_in.astype(BF16), w_kv.astype(BF16))


COMBINE_ROWS = 256


DIL_BLOCKS_PER_ITER = 16


def _softmax_parts(parts):
    m = parts[0]
    for t in parts[1:]:
        m = jnp.maximum(m, t)
    m = jnp.max(m, axis=-1, keepdims=True)
    return [jnp.exp(t - m) for t in parts], m


def _cross_attn_kernel(qkv0, qkv1, qkv2, kvm_ref, bias_first_ref, bias_ref, y0, y1, y2, om_ref,
                       o_cls, lse_cls, stage):
    s_len = qkv0.shape[0]
    head0 = _head0_lanes()
    gw = GROUP_WIDTH

    def block(g, qkv, p, row0, first):
        qc = slice(p * LANES, (p + 1) * LANES)
        kc = slice(gw + p * LANES, gw + (p + 1) * LANES)
        vc = slice(2 * gw + p * LANES, 2 * gw + (p + 1) * LANES)
        cur = pl.ds(pl.multiple_of(row0, BLK), BLK)
        q = qkv[cur, qc] * QK_SCALE
        kst = _split_heads(qkv[cur, kc], head0)
        vst = _with_den_cols(_split_heads(qkv[cur, vc], head0))
        if first:
            z = lax.dot_general(q, kst, NT_DIMS, preferred_element_type=F32) + bias_first_ref[g, p]
            heads = [[z[:, h * BLK:(h + 1) * BLK]] for h in range(2)]
        else:
            prev = pl.ds(pl.multiple_of(row0 - BLK, BLK), BLK)
            kst = jnp.concatenate([_split_heads(qkv[prev, kc], head0), kst], axis=0)
            vst = jnp.concatenate([_with_den_cols(_split_heads(qkv[prev, vc], head0)), vst], axis=0)
            z = lax.dot_general(q, kst, NT_DIMS, preferred_element_type=F32) + bias_ref[g, p]
            heads = [[z[:, h * BLK:(h + 1) * BLK], z[:, (2 + h) * BLK:(3 + h) * BLK]] for h in range(2)]
        (e0, m0), (e1, m1) = [_softmax_parts(parts) for parts in heads]
        pmat = jnp.concatenate([x for pair in zip(e0, e1) for x in pair], axis=1).astype(BF16)
        od = jnp.dot(pmat, vst, preferred_element_type=F32)
        o_cls[g, cur, qc] = od[:, :LANES] / od[:, LANES:]
        lse_cls[g, cur, qc] = jnp.where(head0, m0, m1) + jnp.log(od[:, LANES:])

    for g, ((_, d), qkv) in enumerate(zip(DIL_GROUPS, (qkv0, qkv1, qkv2))):
        blocks_per_class = s_len // d // BLK
        n_iters = s_len // BLK // DIL_BLOCKS_PER_ITER

        def blocks(it, starts_class, g=g, qkv=qkv, blocks_per_class=blocks_per_class):
            for k in range(DIL_BLOCKS_PER_ITER):
                if blocks_per_class <= DIL_BLOCKS_PER_ITER:
                    first = k % blocks_per_class == 0
                else:
                    first = k == 0 and starts_class
                for p in range(gw // LANES):
                    block(g, qkv, p, (it * DIL_BLOCKS_PER_ITER + k) * BLK, first)

        blocks(0, True)
        if n_iters > 1:
            lax.fori_loop(1, n_iters, lambda it, x, blocks=blocks: (blocks(it, False), x)[1], 0)

    def natural_rows(src, g, c, k):
        d = DIL_GROUPS[g][1]
        if d == 1:
            return src[g, pl.ds(pl.multiple_of(c * COMBINE_ROWS, COMBINE_ROWS), COMBINE_ROWS), :]
        n = COMBINE_ROWS // d
        n_cols = gw // LANES
        for r in range(d):
            rows = pl.ds(pl.multiple_of(r * (s_len // d) + c * n, n), n)
            for j in range(n_cols):
                stage[k * n_cols + j, pl.ds(r, n, stride=d), :] = src[g, rows, j * LANES:(j + 1) * LANES]
        return jnp.concatenate([stage[k * n_cols + j] for j in range(n_cols)], axis=1)

    def combine(c, carry):
        ls = [natural_rows(lse_cls, g, c, g) for g in range(3)]
        m = jnp.maximum(jnp.maximum(ls[0], ls[1]), ls[2])
        es = [jnp.exp(l - m) for l in ls]
        tot = es[0] + es[1] + es[2]
        rows = pl.ds(pl.multiple_of(c * COMBINE_ROWS, COMBINE_ROWS), COMBINE_ROWS)
        for g, y_ref in enumerate((y0, y1, y2)):
            y_ref[rows, :] = (natural_rows(o_cls, g, c, 3 + g) * (es[g] / tot)).astype(y_ref.dtype)
        return carry

    lax.fori_loop(0, s_len // COMBINE_ROWS, combine, 0)
    _mem_heads(qkv0.at[:, 3 * gw:], kvm_ref, om_ref)


def _dil_biases(slopes):
    i = jnp.arange(BLK)[:, None]
    j = jnp.arange(2 * BLK)[None, :]
    delta = i + BLK - j
    valid = (delta >= 0) & (delta <= BLK)
    firsts, others = [], []
    for g, (_, d) in enumerate(DIL_GROUPS):
        dist = (delta * d).astype(F32)
        f_p, o_p = [], []
        for p in range(2):
            halves = [jnp.where(valid, -slopes[g * HEADS_PER_GROUP + 2 * p + h] * dist, -jnp.inf) for h in range(2)]
            f_p.append(jnp.concatenate([halves[0][:, BLK:], halves[1][:, BLK:]], axis=1))
            o_p.append(jnp.concatenate([halves[0][:, :BLK], halves[1][:, :BLK],
                                        halves[0][:, BLK:], halves[1][:, BLK:]], axis=1))
        firsts.append(jnp.stack(f_p))
        others.append(jnp.stack(o_p))
    return jnp.stack(firsts), jnp.stack(others)


def _cross_attn(qkvs, kv_mem, batch, s_len):
    mem_len = kv_mem.shape[0] // batch
    slopes = 2.0 ** (-ALIBI_MAX_BIAS * jnp.arange(1, N_DIL_HEADS + 1, dtype=F32) / N_DIL_HEADS)
    bias_first, bias = _dil_biases(slopes)
    seq = lambda n: pl.BlockSpec((s_len, n), lambda b: (b, 0))
    n_out = len(DIL_GROUPS) + 1
    return pl.pallas_call(
        _cross_attn_kernel,
        grid=(batch,),
        in_specs=[seq(a.shape[1]) for a in qkvs]
        + [pl.BlockSpec((mem_len, 2 * MEM_WIDTH), lambda b: (b, 0)), _resident(bias_first.shape), _resident(bias.shape)],
        out_specs=[seq(GROUP_WIDTH)] * n_out,
        out_shape=[jax.ShapeDtypeStruct((batch * s_len, GROUP_WIDTH), BF16)] * n_out,
        scratch_shapes=[pltpu.VMEM((len(DIL_GROUPS), s_len, GROUP_WIDTH), F32)] * 2
        + [pltpu.VMEM((2 * len(DIL_GROUPS) * GROUP_WIDTH // LANES, COMBINE_ROWS, LANES), F32)],
        compiler_params=_params(("parallel",)),
        name="cross_attn",
    )(*qkvs, kv_mem, bias_first, bias)


SUBLANES = 8
FFN_ROWS = 512
FFN_CHUNK = 256
FFN_VREG_ROWS = FFN_ROWS // SUBLANES
FFN_PITCH = FFN_VREG_ROWS + SUBLANES


def _out_ffn_kernel(n_attn, tiles_per_seq, final, h_ref, *refs):
    a_refs, wo_refs = refs[:n_attn], refs[n_attn:2 * n_attn]
    g_ref, wup_ref, wconv_ref, wdown_ref = refs[2 * n_attn:2 * n_attn + 4]
    rest = refs[2 * n_attn + 4:]
    if final:
        gf_ref, out_ref, stage_ref, xe_ref, act_ref, tail_ref = rest
    else:
        out_ref, stage_ref, xe_ref, act_ref, tail_ref = rest
    n_cols = h_ref.shape[1] // LANES
    nv = FFN_VREG_ROWS
    i = pl.program_id(0)

    @pl.when(i == 0)
    def _():
        tail_ref[...] = jnp.zeros_like(tail_ref)

    h = h_ref[...]
    for a_ref, wo_ref in zip(a_refs, wo_refs):
        h = h + jnp.dot(a_ref[...], wo_ref[...], preferred_element_type=F32)
    xn = _rms_scale(h) * g_ref[...]

    for c in range(n_cols):
        for s in range(SUBLANES):
            stage_ref[c, s * FFN_PITCH:s * FFN_PITCH + nv, :] = xn[s * nv:(s + 1) * nv, c * LANES:(c + 1) * LANES]
    for jj in range(nv * SUBLANES // BF16_ROWS):
        vregs = [jnp.concatenate([stage_ref[c, pl.ds(j, SUBLANES, stride=FFN_PITCH), :] for c in range(n_cols)], axis=1)
                 for j in range(jj * BF16_ROWS // SUBLANES, (jj + 1) * BF16_ROWS // SUBLANES)]
        xe_ref[jj * BF16_ROWS:(jj + 1) * BF16_ROWS, :] = jnp.concatenate(vregs, axis=0).astype(xe_ref.dtype)

    starts_seq = i % tiles_per_seq == 0
    first_sublane = lax.broadcasted_iota(jnp.int32, (SUBLANES, FFN_CHUNK), 0) == 0

    def conv(u, wc, tail):
        def before_first(prev_vreg, last_vreg):
            return jnp.where(first_sublane, pltpu.roll(prev_vreg, 1, axis=0), pltpu.roll(last_vreg, 1, axis=0))

        back1 = before_first(tail[SUBLANES:], u[-SUBLANES:])
        back2 = before_first(tail[:SUBLANES], u[-2 * SUBLANES:-SUBLANES])
        u1 = jnp.concatenate([back1, u[:-SUBLANES]], axis=0)
        u2 = jnp.concatenate([back2, back1, u[:-2 * SUBLANES]], axis=0)
        return wc[0:1] * u2 + wc[1:2] * u1 + wc[2:3] * u

    xe = xe_ref[...]
    for c in range(D_FF // FFN_CHUNK):
        convs = []
        for part in range(2):
            cols = slice(part * D_FF + c * FFN_CHUNK, part * D_FF + (c + 1) * FFN_CHUNK)
            u = jnp.dot(xe, wup_ref[:, cols], preferred_element_type=F32)
            tail = tail_ref[2 * c + part]
            tail_ref[2 * c + part] = u[-2 * SUBLANES:]
            convs.append(conv(u, wconv_ref[:, cols], jnp.where(starts_seq, jnp.zeros_like(tail), tail)))
        half = 0.5 * convs[1]
        act_ref[:, c * FFN_CHUNK:(c + 1) * FFN_CHUNK] = ((half + half * jnp.tanh(half)) * convs[0]).astype(act_ref.dtype)

    y = jnp.dot(act_ref[...], wdown_ref[...], preferred_element_type=F32)
    for c in range(n_cols):
        for j in range(nv):
            stage_ref[c, pl.ds(j, SUBLANES, stride=FFN_PITCH), :] = y[j * SUBLANES:(j + 1) * SUBLANES,
                                                                      c * LANES:(c + 1) * LANES]
    y = h + jnp.concatenate(
        [jnp.concatenate([stage_ref[c, s * FFN_PITCH:s * FFN_PITCH + nv, :] for c in range(n_cols)], axis=1)
         for s in range(SUBLANES)], axis=0)
    if final:
        y = _rms_scale(y) * gf_ref[...]
    out_ref[...] = y


def _out_ffn(h, attn_parts, w_out, g_ffn, w_up, w_conv, w_down, s_len, g_final=None):
    t, d = h.shape
    tm = FFN_ROWS
    n_attn = len(attn_parts)
    final = g_final is not None
    offs = [0]
    for a in attn_parts:
        offs.append(offs[-1] + a.shape[1])
    wo_parts = [w_out[offs[k]:offs[k + 1]].astype(BF16) for k in range(n_attn)]
    row = lambda n: pl.BlockSpec((tm, n), lambda i: (i, 0))
    in_specs = [row(d)] + [row(a.shape[1]) for a in attn_parts] + [_resident(w.shape) for w in wo_parts]
    in_specs += [_resident((1, d)), _resident(w_up.shape), _resident(w_conv.shape), _resident(w_down.shape)]
    args = [h, *attn_parts, *wo_parts, g_ffn.reshape(1, d), w_up.astype(BF16), w_conv, w_down.astype(BF16)]
    if final:
        in_specs.append(_resident((1, d)))
        args.append(g_final.reshape(1, d))
    return pl.pallas_call(
        functools.partial(_out_ffn_kernel, n_attn, s_len // tm, final),
        grid=(t // tm,),
        in_specs=in_specs,
        out_specs=row(d),
        out_shape=jax.ShapeDtypeStruct((t, d), F32),
        scratch_shapes=[pltpu.VMEM((d // LANES, SUBLANES * FFN_PITCH, LANES), F32), pltpu.VMEM((tm, d), BF16),
                        pltpu.VMEM((tm, D_FF), BF16),
                        pltpu.VMEM((2 * D_FF // FFN_CHUNK, 2 * SUBLANES, FFN_CHUNK), F32)],
        compiler_params=_params(("arbitrary",)),
        name="out_ffn",
    )(*args)


def kernel(x, mem, a_norm_attn, a_w_in, a_w_out, a_norm_mem, a_w_mem_kv, a_norm_ffn, a_ffn_up, a_ffn_conv, a_ffn_down, kv_norm, w_kv_shared, b_norm_attn, b_w_in, b_w_out, b_norm_mem, b_w_mem_kv, b_norm_ffn, b_ffn_up, b_ffn_conv, b_ffn_down, final_norm):
    batch, s_len, d = x.shape
    assert a_w_in.shape[0] == 1 and b_w_in.shape[0] == 1, "one self-decoder and one cross-decoder layer"
    assert d == D_MODEL and s_len % (BLK * DIL_GROUPS[-1][1]) == 0 and s_len % FFN_ROWS == 0
    t = batch * s_len
    h = x.reshape(t, d)
    mem2 = mem.reshape(batch * mem.shape[1], d)

    kvm_a, kvm_b = _rms_proj(mem2, [a_norm_mem[0], b_norm_mem[0]],
                             [a_w_mem_kv[0].astype(BF16), b_w_mem_kv[0].astype(BF16)], PROJ_ROWS)

    (proj_a,) = _rms_proj(h, [a_norm_attn[0]], [a_w_in[0].astype(BF16)], PROJ_ROWS)
    o_sb = _sb_attn(proj_a, batch, s_len)
    o_mem = _mem_attn(proj_a, 3 * SB_WIDTH // MEM_WIDTH, kvm_a, batch, s_len)
    h = _out_ffn(h, [o_sb, o_mem], a_w_out[0], a_norm_ffn[0], a_ffn_up[0], a_ffn_conv[0], a_ffn_down[0], s_len)

    qkvs = _proj_b(h, kv_norm, b_norm_attn[0], w_kv_shared, b_w_in[0], batch, s_len)
    attn_parts = _cross_attn(qkvs, kvm_b, batch, s_len)
    h = _out_ffn(h, attn_parts, b_w_out[0], b_norm_ffn[0], b_ffn_up[0], b_ffn_conv[0], b_ffn_down[0], s_len,
                 g_final=final_norm)
    return h.reshape(batch, s_len, d)
```

```python
import functools

import jax
import jax.numpy as jnp
from jax import lax
from jax.experimental import pallas as pl
from jax.experimental.pallas import tpu as pltpu

D_MODEL = 1024
HEAD_DIM = 64
N_SB_HEADS = 12
N_MEM_HEADS = 4
DIL_GROUPS = ((128, 1), (512, 4), (2048, 16))
HEADS_PER_GROUP = 4
N_DIL_HEADS = HEADS_PER_GROUP * len(DIL_GROUPS)
SB_WIDTH = N_SB_HEADS * HEAD_DIM
MEM_WIDTH = N_MEM_HEADS * HEAD_DIM
DIL_WIDTH = N_DIL_HEADS * HEAD_DIM
GROUP_WIDTH = HEADS_PER_GROUP * HEAD_DIM
D_FF = 2816
CONV_WIDTH = 3
EPS = 1e-6
ALIBI_MAX_BIAS = 8.0
QK_SCALE = HEAD_DIM ** -0.5
LOG2E = 1.4426950408889634

LANES = 128
BF16_ROWS = 16
VMEM_LIMIT_BYTES = 56 * 1024 * 1024

BLK = 128
F32 = jnp.float32
BF16 = jnp.bfloat16
NT_DIMS = (((1,), (1,)), ((), ()))


def _params(semantics):
    return pltpu.CompilerParams(dimension_semantics=semantics, vmem_limit_bytes=VMEM_LIMIT_BYTES)


def _resident(shape):
    return pl.BlockSpec(shape, lambda *_: (0,) * len(shape), pipeline_mode=pl.Buffered(1))


def _head0_lanes():
    return lax.broadcasted_iota(jnp.int32, (1, LANES), 1) < HEAD_DIM


def _split_heads(t, head0):
    zero = jnp.zeros_like(t)
    return jnp.concatenate([jnp.where(head0, t, zero), jnp.where(head0, zero, t)], axis=0)


def _with_den_cols(vst):
    row = lax.broadcasted_iota(jnp.int32, vst.shape, 0)
    lane = lax.broadcasted_iota(jnp.int32, vst.shape, 1)
    owns = (row < vst.shape[0] // 2) == (lane < HEAD_DIM)
    return jnp.concatenate([vst, jnp.where(owns, 1.0, 0.0).astype(vst.dtype)], axis=1)


def _rms_scale(x):
    return x * lax.rsqrt(jnp.mean(x * x, axis=-1, keepdims=True) + EPS)


PROJ_ROWS = 1024


def _rms_proj_kernel(n_out, x_ref, *refs):
    g_refs, w_refs, o_refs = refs[:n_out], refs[n_out:2 * n_out], refs[2 * n_out:]
    xhat = _rms_scale(x_ref[...])
    for g_ref, w_ref, o_ref in zip(g_refs, w_refs, o_refs):
        xn = (xhat * g_ref[...]).astype(BF16)
        o_ref[...] = jnp.dot(xn, w_ref[...], preferred_element_type=F32).astype(o_ref.dtype)


def _rms_proj(x, gains, weights, tm):
    t, d = x.shape
    n_out = len(gains)
    in_specs = [pl.BlockSpec((tm, d), lambda i: (i, 0))]
    in_specs += [_resident((1, d)) for _ in gains]
    in_specs += [_resident(w.shape) for w in weights]
    out_specs = [pl.BlockSpec((tm, w.shape[1]), lambda i: (i, 0)) for w in weights]
    out_shape = [jax.ShapeDtypeStruct((t, w.shape[1]), BF16) for w in weights]
    return pl.pallas_call(
        functools.partial(_rms_proj_kernel, n_out),
        grid=(t // tm,),
        in_specs=in_specs,
        out_specs=out_specs,
        out_shape=out_shape,
        compiler_params=_params(("parallel",)),
        name="rms_proj",
    )(x, *[g.reshape(1, d) for g in gains], *weights)


SB_QBLKS = 16
SB_ROWS = SB_QBLKS * BLK
SB_PAIRS = 6
SB_BAND = 3
SB_UNDERFLOW = 106.0
assert SB_BAND < SB_QBLKS


def _sb_kernel(q_ref, k_ref, v_ref, tri_ref, o_ref, carry_ref, acc_ref, lowest_ref):
    qt = pl.program_id(2)
    head0 = _head0_lanes()
    tri = tri_ref[...]
    pairs = range(SB_PAIRS)
    q_all = [q_ref[:, p * LANES:(p + 1) * LANES] * QK_SCALE for p in pairs]

    def key_block(p, k0, row_lo, row_hi, diagonal, carry, acc, live_from=None):
        cols = slice(p * LANES, (p + 1) * LANES)
        kst = _split_heads(k_ref[pl.ds(k0, BLK), cols], head0)
        vst = _split_heads(v_ref[pl.ds(k0, BLK), cols], head0)
        z = lax.dot_general(q_all[p][row_lo:row_hi], kst, NT_DIMS, preferred_element_type=F32)
        sp = jnp.maximum(z, 0.0) + jnp.log(1.0 + jnp.exp2(jnp.abs(z) * (-LOG2E)))
        if diagonal:
            t_rel = lax.broadcasted_iota(jnp.int32, (BLK, 2 * BLK), 0)
            s_rel = lax.broadcasted_iota(jnp.int32, (BLK, 2 * BLK), 1) & (BLK - 1)
            causal = s_rel < t_rel

            def mask(t):
                top = jnp.where(causal, t[:BLK], 0.0)
                return top if t.shape[0] == BLK else jnp.concatenate([top, t[BLK:]], axis=0)
        elif live_from is not None:
            live = lax.broadcasted_iota(jnp.int32, z.shape, 0) + row_lo >= live_from

            def mask(t):
                return jnp.where(live, t, 0.0)
        else:
            def mask(t):
                return t

        sp = mask(sp)
        sp16 = sp.astype(BF16)
        sums = [jnp.dot(sp16[:, h * BLK:(h + 1) * BLK], tri, preferred_element_type=F32)
                for h in range(2)]
        suffix = jnp.concatenate([s[:, :BLK] for s in sums], axis=1)
        total = jnp.concatenate([s[:, BLK:] for s in sums], axis=1)
        w = mask(jnp.exp2((z - (suffix + carry[row_lo:row_hi])) * LOG2E))
        pv = jnp.dot(w.astype(BF16), vst, preferred_element_type=F32)

        def all_rows(t):
            parts = [jnp.zeros((n, t.shape[1]), F32) if n else None for n in (row_lo, SB_ROWS - row_hi)]
            parts = [x for x in (parts[0], t, parts[1]) if x is not None]
            return t if len(parts) == 1 else jnp.concatenate(parts, axis=0)

        return carry + all_rows(total), acc + all_rows(pv)

    r0 = qt * SB_ROWS

    def band(lowest, pending_from):
        carry = [jnp.zeros((SB_ROWS, 2 * BLK), F32) for _ in pairs]
        acc = [jnp.zeros((SB_ROWS, LANES), F32) for _ in pairs]
        for j in range(SB_QBLKS - 1, lowest - 1, -1):
            for p in pairs:
                carry[p], acc[p] = key_block(p, pl.multiple_of(r0 + j * BLK, BLK), max(j, 0) * BLK,
                                             min(j + SB_BAND, SB_QBLKS) * BLK, j >= 0, carry[p], acc[p])
        smallest = None
        for p in pairs:
            carry_ref[p] = carry[p]
            acc_ref[p] = acc[p]
            o_ref[:, p * LANES:(p + 1) * LANES] = acc[p].astype(o_ref.dtype)
            m = jnp.min(carry[p][pending_from:])
            smallest = m if smallest is None else jnp.minimum(smallest, m)
        lowest_ref[0] = smallest

    @pl.when(qt == 0)
    def _():
        band(0, SB_BAND * BLK)

    @pl.when(qt > 0)
    def _():
        band(1 - SB_BAND, 0)

    top = SB_QBLKS - 1 - SB_BAND
    n_steps = top + 1 + qt * SB_QBLKS

    def step(state):
        j = top - state[0]
        lowest = None
        for p in pairs:
            carry, acc = key_block(p, pl.multiple_of(r0 + j * BLK, BLK), 0, SB_ROWS, False, carry_ref[p], acc_ref[p],
                                   live_from=(j + SB_BAND) * BLK)
            carry_ref[p] = carry
            acc_ref[p] = acc
            lowest = jnp.min(carry) if lowest is None else jnp.minimum(lowest, jnp.min(carry))
        return state[0] + 1, lowest

    @pl.when(lowest_ref[0] < SB_UNDERFLOW)
    def _():
        lax.while_loop(lambda s: jnp.logical_and(s[0] < n_steps, s[1] < SB_UNDERFLOW), step,
                       (jnp.int32(0), lowest_ref[0]))
        for p in pairs:
            o_ref[:, p * LANES:(p + 1) * LANES] = acc_ref[p].astype(o_ref.dtype)


def _sb_tri_weights():
    j = jnp.arange(BLK)[:, None]
    c = jnp.arange(2 * BLK)[None, :]
    return ((c >= BLK) | (j >= c)).astype(BF16)


def _sb_attn(proj, batch, s_len):
    width = SB_PAIRS * LANES
    n_groups = SB_WIDTH // width
    n_qt = s_len // SB_ROWS
    kv_blk = (s_len, width)
    return pl.pallas_call(
        _sb_kernel,
        grid=(batch, n_groups, n_qt),
        in_specs=[
            pl.BlockSpec((SB_ROWS, width), lambda b, p, t: (b * n_qt + t, p)),
            pl.BlockSpec(kv_blk, lambda b, p, t: (b, n_groups + p)),
            pl.BlockSpec(kv_blk, lambda b, p, t: (b, 2 * n_groups + p)),
            _resident((BLK, 2 * BLK)),
        ],
        out_specs=pl.BlockSpec((SB_ROWS, width), lambda b, p, t: (b * n_qt + t, p)),
        out_shape=jax.ShapeDtypeStruct((batch * s_len, SB_WIDTH), BF16),
        scratch_shapes=[pltpu.VMEM((SB_PAIRS, SB_ROWS, 2 * BLK), F32), pltpu.VMEM((SB_PAIRS, SB_ROWS, LANES), F32),
                        pltpu.SMEM((1,), F32)],
        compiler_params=_params(("parallel", "parallel", "arbitrary")),
        name="sb_attn",
    )(proj, proj, proj, _sb_tri_weights())


MEM_ROWS = 2048


def _mem_heads(q_ref, kv_ref, o_ref):
    s_len = q_ref.shape[0]
    mem_len = kv_ref.shape[0]
    head0 = _head0_lanes()
    n_pairs = MEM_WIDTH // LANES
    ksts = [_split_heads(kv_ref[:, p * LANES:(p + 1) * LANES], head0) for p in range(n_pairs)]
    vsts = [_with_den_cols(_split_heads(kv_ref[:, MEM_WIDTH + p * LANES:MEM_WIDTH + (p + 1) * LANES], head0))
            for p in range(n_pairs)]

    def step(c, carry):
        rows = pl.ds(pl.multiple_of(c * MEM_ROWS, MEM_ROWS), MEM_ROWS)
        for p in range(n_pairs):
            q = q_ref[rows, p * LANES:(p + 1) * LANES] * QK_SCALE
            z = lax.dot_general(q, ksts[p], NT_DIMS, preferred_element_type=F32)
            es = []
            for h in range(2):
                zh = z[:, h * mem_len:(h + 1) * mem_len]
                es.append(jnp.exp(zh - jnp.max(zh, axis=-1, keepdims=True)))
            od = jnp.dot(jnp.concatenate(es, axis=1).astype(BF16), vsts[p], preferred_element_type=F32)
            o_ref[rows, p * LANES:(p + 1) * LANES] = (od[:, :LANES] / od[:, LANES:]).astype(o_ref.dtype)
        return carry

    lax.fori_loop(0, s_len // MEM_ROWS, step, 0)


def _mem_attn(proj, q_col_block, kv_mem, batch, s_len):
    mem_len = kv_mem.shape[0] // batch
    return pl.pallas_call(
        _mem_heads,
        grid=(batch,),
        in_specs=[
            pl.BlockSpec((s_len, MEM_WIDTH), lambda b: (b, q_col_block)),
            pl.BlockSpec((mem_len, 2 * MEM_WIDTH), lambda b: (b, 0)),
        ],
        out_specs=pl.BlockSpec((s_len, MEM_WIDTH), lambda b: (b, 0)),
        out_shape=jax.ShapeDtypeStruct((batch * s_len, MEM_WIDTH), BF16),
        compiler_params=_params(("parallel",)),
        name="mem_attn",
    )(proj, kv_mem)


def _proj_b_kernel(x_ref, gkv_ref, gq_ref, wq_ref, wkv_ref, o0, o1, o2, stage_ref):
    tm = x_ref.shape[0]
    s_len = o0.shape[0]
    c = pl.program_id(1)
    gw = GROUP_WIDTH
    xhat = _rms_scale(x_ref[...])
    q = jnp.dot((xhat * gq_ref[...]).astype(BF16), wq_ref[...], preferred_element_type=F32)
    kv = jnp.dot((xhat * gkv_ref[...]).astype(BF16), wkv_ref[...], preferred_element_type=F32)
    for g, ((_, d), o_ref) in enumerate(zip(DIL_GROUPS, (o0, o1, o2))):
        qkv = jnp.concatenate([q[:, g * gw:(g + 1) * gw], kv[:, g * gw:(g + 1) * gw],
                               kv[:, DIL_WIDTH + g * gw:DIL_WIDTH + (g + 1) * gw]], axis=1)
        if d == 1:
            rows = pl.ds(pl.multiple_of(c * tm, tm), tm)
            o_ref[rows, :3 * gw] = qkv.astype(BF16)
            o_ref[rows, 3 * gw:] = q[:, DIL_WIDTH:].astype(BF16)
        else:
            n_cols = 3 * gw // LANES
            for j in range(n_cols):
                stage_ref[j] = qkv[:, j * LANES:(j + 1) * LANES]
            n = tm // d
            for r in range(d):
                dst = pl.ds(pl.multiple_of(r * (s_len // d) + c * n, n), n)
                o_ref[dst, :] = jnp.concatenate(
                    [stage_ref[j, pl.ds(r, n, stride=d), :] for j in range(n_cols)], axis=1).astype(BF16)


def _proj_b(h, g_kv, g_q, w_kv, w_in, batch, s_len):
    d = h.shape[1]
    tm = PROJ_ROWS
    n_tiles = s_len // tm
    widths = [3 * GROUP_WIDTH + MEM_WIDTH, 3 * GROUP_WIDTH, 3 * GROUP_WIDTH]
    return pl.pallas_call(
        _proj_b_kernel,
        grid=(batch, n_tiles),
        in_specs=[pl.BlockSpec((tm, d), lambda b, c: (b * n_tiles + c, 0)), _resident((1, d)), _resident((1, d)),
                  _resident(w_in.shape), _resident(w_kv.shape)],
        out_specs=[pl.BlockSpec((s_len, n), lambda b, c: (b, 0)) for n in widths],
        out_shape=[jax.ShapeDtypeStruct((batch * s_len, n), BF16) for n in widths],
        scratch_shapes=[pltpu.VMEM((3 * GROUP_WIDTH // LANES, tm, LANES), F32)],
        compiler_params=_params(("parallel", "arbitrary")),
        name="proj_b",
    )(h, g_kv.reshape(1, d), g_q.reshape(1, d), w_in.astype(BF16), w_kv.astype(BF16))


COMBINE_ROWS = 256


DIL_BLOCKS_PER_ITER = 16


def _softmax_parts(parts):
    m = parts[0]
    for t in parts[1:]:
        m = jnp.maximum(m, t)
    m = jnp.max(m, axis=-1, keepdims=True)
    return [jnp.exp(t - m) for t in parts], m


def _cross_attn_kernel(qkv0, qkv1, qkv2, kvm_ref, bias_first_ref, bias_ref, y0, y1, y2, om_ref,
                       o_cls, lse_cls, stage):
    s_len = qkv0.shape[0]
    head0 = _head0_lanes()
    gw = GROUP_WIDTH

    def block(g, qkv, p, row0, first):
        qc = slice(p * LANES, (p + 1) * LANES)
        kc = slice(gw + p * LANES, gw + (p + 1) * LANES)
        vc = slice(2 * gw + p * LANES, 2 * gw + (p + 1) * LANES)
        cur = pl.ds(pl.multiple_of(row0, BLK), BLK)
        q = qkv[cur, qc] * QK_SCALE
        kst = _split_heads(qkv[cur, kc], head0)
        vst = _with_den_cols(_split_heads(qkv[cur, vc], head0))
        if first:
            z = lax.dot_general(q, kst, NT_DIMS, preferred_element_type=F32) + bias_first_ref[g, p]
            heads = [[z[:, h * BLK:(h + 1) * BLK]] for h in range(2)]
        else:
            prev = pl.ds(pl.multiple_of(row0 - BLK, BLK), BLK)
            kst = jnp.concatenate([_split_heads(qkv[prev, kc], head0), kst], axis=0)
            vst = jnp.concatenate([_with_den_cols(_split_heads(qkv[prev, vc], head0)), vst], axis=0)
            z = lax.dot_general(q, kst, NT_DIMS, preferred_element_type=F32) + bias_ref[g, p]
            heads = [[z[:, h * BLK:(h + 1) * BLK], z[:, (2 + h) * BLK:(3 + h) * BLK]] for h in range(2)]
        (e0, m0), (e1, m1) = [_softmax_parts(parts) for parts in heads]
        pmat = jnp.concatenate([x for pair in zip(e0, e1) for x in pair], axis=1).astype(BF16)
        od = jnp.dot(pmat, vst, preferred_element_type=F32)
        o_cls[g, cur, qc] = od[:, :LANES] / od[:, LANES:]
        lse_cls[g, cur, qc] = jnp.where(head0, m0, m1) + jnp.log(od[:, LANES:])

    for g, ((_, d), qkv) in enumerate(zip(DIL_GROUPS, (qkv0, qkv1, qkv2))):
        blocks_per_class = s_len // d // BLK
        n_iters = s_len // BLK // DIL_BLOCKS_PER_ITER

        def blocks(it, starts_class, g=g, qkv=qkv, blocks_per_class=blocks_per_class):
            for k in range(DIL_BLOCKS_PER_ITER):
                if blocks_per_class <= DIL_BLOCKS_PER_ITER:
                    first = k % blocks_per_class == 0
                else:
                    first = k == 0 and starts_class
                for p in range(gw // LANES):
                    block(g, qkv, p, (it * DIL_BLOCKS_PER_ITER + k) * BLK, first)

        blocks(0, True)
        if n_iters > 1:
            lax.fori_loop(1, n_iters, lambda it, x, blocks=blocks: (blocks(it, False), x)[1], 0)

    def natural_rows(src, g, c, k):
        d = DIL_GROUPS[g][1]
        if d == 1:
            return src[g, pl.ds(pl.multiple_of(c * COMBINE_ROWS, COMBINE_ROWS), COMBINE_ROWS), :]
        n = COMBINE_ROWS // d
        n_cols = gw // LANES
        for r in range(d):
            rows = pl.ds(pl.multiple_of(r * (s_len // d) + c * n, n), n)
            for j in range(n_cols):
                stage[k * n_cols + j, pl.ds(r, n, stride=d), :] = src[g, rows, j * LANES:(j + 1) * LANES]
        return jnp.concatenate([stage[k * n_cols + j] for j in range(n_cols)], axis=1)

    def combine(c, carry):
        ls = [natural_rows(lse_cls, g, c, g) for g in range(3)]
        m = jnp.maximum(jnp.maximum(ls[0], ls[1]), ls[2])
        es = [jnp.exp(l - m) for l in ls]
        tot = es[0] + es[1] + es[2]
        rows = pl.ds(pl.multiple_of(c * COMBINE_ROWS, COMBINE_ROWS), COMBINE_ROWS)
        for g, y_ref in enumerate((y0, y1, y2)):
            y_ref[rows, :] = (natural_rows(o_cls, g, c, 3 + g) * (es[g] / tot)).astype(y_ref.dtype)
        return carry

    lax.fori_loop(0, s_len // COMBINE_ROWS, combine, 0)
    _mem_heads(qkv0.at[:, 3 * gw:], kvm_ref, om_ref)


def _dil_biases(slopes):
    i = jnp.arange(BLK)[:, None]
    j = jnp.arange(2 * BLK)[None, :]
    delta = i + BLK - j
    valid = (delta >= 0) & (delta <= BLK)
    firsts, others = [], []
    for g, (_, d) in enumerate(DIL_GROUPS):
        dist = (delta * d).astype(F32)
        f_p, o_p = [], []
        for p in range(2):
            halves = [jnp.where(valid, -slopes[g * HEADS_PER_GROUP + 2 * p + h] * dist, -jnp.inf) for h in range(2)]
            f_p.append(jnp.concatenate([halves[0][:, BLK:], halves[1][:, BLK:]], axis=1))
            o_p.append(jnp.concatenate([halves[0][:, :BLK], halves[1][:, :BLK],
                                        halves[0][:, BLK:], halves[1][:, BLK:]], axis=1))
        firsts.append(jnp.stack(f_p))
        others.append(jnp.stack(o_p))
    return jnp.stack(firsts), jnp.stack(others)


def _cross_attn(qkvs, kv_mem, batch, s_len):
    mem_len = kv_mem.shape[0] // batch
    slopes = 2.0 ** (-ALIBI_MAX_BIAS * jnp.arange(1, N_DIL_HEADS + 1, dtype=F32) / N_DIL_HEADS)
    bias_first, bias = _dil_biases(slopes)
    seq = lambda n: pl.BlockSpec((s_len, n), lambda b: (b, 0))
    n_out = len(DIL_GROUPS) + 1
    return pl.pallas_call(
        _cross_attn_kernel,
        grid=(batch,),
        in_specs=[seq(a.shape[1]) for a in qkvs]
        + [pl.BlockSpec((mem_len, 2 * MEM_WIDTH), lambda b: (b, 0)), _resident(bias_first.shape), _resident(bias.shape)],
        out_specs=[seq(GROUP_WIDTH)] * n_out,
        out_shape=[jax.ShapeDtypeStruct((batch * s_len, GROUP_WIDTH), BF16)] * n_out,
        scratch_shapes=[pltpu.VMEM((len(DIL_GROUPS), s_len, GROUP_WIDTH), F32)] * 2
        + [pltpu.VMEM((2 * len(DIL_GROUPS) * GROUP_WIDTH // LANES, COMBINE_ROWS, LANES), F32)],
        compiler_params=_params(("parallel",)),
        name="cross_attn",
    )(*qkvs, kv_mem, bias_first, bias)


SUBLANES = 8
FFN_ROWS = 512
FFN_CHUNK = 256
FFN_VREG_ROWS = FFN_ROWS // SUBLANES
FFN_PITCH = FFN_VREG_ROWS + SUBLANES


def _out_ffn_kernel(n_attn, tiles_per_seq, final, h_ref, *refs):
    a_refs, wo_refs = refs[:n_attn], refs[n_attn:2 * n_attn]
    g_ref, wup_ref, wconv_ref, wdown_ref = refs[2 * n_attn:2 * n_attn + 4]
    rest = refs[2 * n_attn + 4:]
    if final:
        gf_ref, out_ref, stage_ref, xe_ref, act_ref, tail_ref = rest
    else:
        out_ref, stage_ref, xe_ref, act_ref, tail_ref = rest
    n_cols = h_ref.shape[1] // LANES
    nv = FFN_VREG_ROWS
    i = pl.program_id(0)

    @pl.when(i == 0)
    def _():
        tail_ref[...] = jnp.zeros_like(tail_ref)

    h = h_ref[...]
    for a_ref, wo_ref in zip(a_refs, wo_refs):
        h = h + jnp.dot(a_ref[...], wo_ref[...], preferred_element_type=F32)
    xn = _rms_scale(h) * g_ref[...]

    for c in range(n_cols):
        for s in range(SUBLANES):
            stage_ref[c, s * FFN_PITCH:s * FFN_PITCH + nv, :] = xn[s * nv:(s + 1) * nv, c * LANES:(c + 1) * LANES]
    for jj in range(nv * SUBLANES // BF16_ROWS):
        vregs = [jnp.concatenate([stage_ref[c, pl.ds(j, SUBLANES, stride=FFN_PITCH), :] for c in range(n_cols)], axis=1)
                 for j in range(jj * BF16_ROWS // SUBLANES, (jj + 1) * BF16_ROWS // SUBLANES)]
        xe_ref[jj * BF16_ROWS:(jj + 1) * BF16_ROWS, :] = jnp.concatenate(vregs, axis=0).astype(xe_ref.dtype)

    starts_seq = i % tiles_per_seq == 0
    first_sublane = lax.broadcasted_iota(jnp.int32, (SUBLANES, FFN_CHUNK), 0) == 0

    def conv(u, wc, tail):
        def before_first(prev_vreg, last_vreg):
            return jnp.where(first_sublane, pltpu.roll(prev_vreg, 1, axis=0), pltpu.roll(last_vreg, 1, axis=0))

        back1 = before_first(tail[SUBLANES:], u[-SUBLANES:])
        back2 = before_first(tail[:SUBLANES], u[-2 * SUBLANES:-SUBLANES])
        u1 = jnp.concatenate([back1, u[:-SUBLANES]], axis=0)
        u2 = jnp.concatenate([back2, back1, u[:-2 * SUBLANES]], axis=0)
        return wc[0:1] * u2 + wc[1:2] * u1 + wc[2:3] * u

    xe = xe_ref[...]
    for c in range(D_FF // FFN_CHUNK):
        convs = []
        for part in range(2):
            cols = slice(part * D_FF + c * FFN_CHUNK, part * D_FF + (c + 1) * FFN_CHUNK)
            u = jnp.dot(xe, wup_ref[:, cols], preferred_element_type=F32)
            tail = tail_ref[2 * c + part]
            tail_ref[2 * c + part] = u[-2 * SUBLANES:]
            convs.append(conv(u, wconv_ref[:, cols], jnp.where(starts_seq, jnp.zeros_like(tail), tail)))
        half = 0.5 * convs[1]
        act_ref[:, c * FFN_CHUNK:(c + 1) * FFN_CHUNK] = ((half + half * jnp.tanh(half)) * convs[0]).astype(act_ref.dtype)

    y = jnp.dot(act_ref[...], wdown_ref[...], preferred_element_type=F32)
    for c in range(n_cols):
        for j in range(nv):
            stage_ref[c, pl.ds(j, SUBLANES, stride=FFN_PITCH), :] = y[j * SUBLANES:(j + 1) * SUBLANES,
                                                                      c * LANES:(c + 1) * LANES]
    y = h + jnp.concatenate(
        [jnp.concatenate([stage_ref[c, s * FFN_PITCH:s * FFN_PITCH + nv, :] for c in range(n_cols)], axis=1)
         for s in range(SUBLANES)], axis=0)
    if final:
        y = _rms_scale(y) * gf_ref[...]
    out_ref[...] = y


def _out_ffn(h, attn_parts, w_out, g_ffn, w_up, w_conv, w_down, s_len, g_final=None):
    t, d = h.shape
    tm = FFN_ROWS
    n_attn = len(attn_parts)
    final = g_final is not None
    offs = [0]
    for a in attn_parts:
        offs.append(offs[-1] + a.shape[1])
    wo_parts = [w_out[offs[k]:offs[k + 1]].astype(BF16) for k in range(n_attn)]
    row = lambda n: pl.BlockSpec((tm, n), lambda i: (i, 0))
    in_specs = [row(d)] + [row(a.shape[1]) for a in attn_parts] + [_resident(w.shape) for w in wo_parts]
    in_specs += [_resident((1, d)), _resident(w_up.shape), _resident(w_conv.shape), _resident(w_down.shape)]
    args = [h, *attn_parts, *wo_parts, g_ffn.reshape(1, d), w_up.astype(BF16), w_conv, w_down.astype(BF16)]
    if final:
        in_specs.append(_resident((1, d)))
        args.append(g_final.reshape(1, d))
    return pl.pallas_call(
        functools.partial(_out_ffn_kernel, n_attn, s_len // tm, final),
        grid=(t // tm,),
        in_specs=in_specs,
        out_specs=row(d),
        out_shape=jax.ShapeDtypeStruct((t, d), F32),
        scratch_shapes=[pltpu.VMEM((d // LANES, SUBLANES * FFN_PITCH, LANES), F32), pltpu.VMEM((tm, d), BF16),
                        pltpu.VMEM((tm, D_FF), BF16),
                        pltpu.VMEM((2 * D_FF // FFN_CHUNK, 2 * SUBLANES, FFN_CHUNK), F32)],
        compiler_params=_params(("arbitrary",)),
        name="out_ffn",
    )(*args)


def kernel(x, mem, a_norm_attn, a_w_in, a_w_out, a_norm_mem, a_w_mem_kv, a_norm_ffn, a_ffn_up, a_ffn_conv, a_ffn_down, kv_norm, w_kv_shared, b_norm_attn, b_w_in, b_w_out, b_norm_mem, b_w_mem_kv, b_norm_ffn, b_ffn_up, b_ffn_conv, b_ffn_down, final_norm):
    batch, s_len, d = x.shape
    assert a_w_in.shape[0] == 1 and b_w_in.shape[0] == 1, "one self-decoder and one cross-decoder layer"
    assert d == D_MODEL and s_len % (BLK * DIL_GROUPS[-1][1]) == 0 and s_len % FFN_ROWS == 0
    assert s_len % SB_ROWS == 0 and s_len % MEM_ROWS == 0 and s_len % PROJ_ROWS == 0
    assert (batch * mem.shape[1]) % PROJ_ROWS == 0
    t = batch * s_len
    h = x.reshape(t, d)
    mem2 = mem.reshape(batch * mem.shape[1], d)

    kvm_a, kvm_b = _rms_proj(mem2, [a_norm_mem[0], b_norm_mem[0]],
                             [a_w_mem_kv[0].astype(BF16), b_w_mem_kv[0].astype(BF16)], PROJ_ROWS)

    (proj_a,) = _rms_proj(h, [a_norm_attn[0]], [a_w_in[0].astype(BF16)], PROJ_ROWS)
    o_sb = _sb_attn(proj_a, batch, s_len)
    o_mem = _mem_attn(proj_a, 3 * SB_WIDTH // MEM_WIDTH, kvm_a, batch, s_len)
    h = _out_ffn(h, [o_sb, o_mem], a_w_out[0], a_norm_ffn[0], a_ffn_up[0], a_ffn_conv[0], a_ffn_down[0], s_len)

    qkvs = _proj_b(h, kv_norm, b_norm_attn[0], w_kv_shared, b_w_in[0], batch, s_len)
    attn_parts = _cross_attn(qkvs, kvm_b, batch, s_len)
    h = _out_ffn(h, attn_parts, b_w_out[0], b_norm_ffn[0], b_ffn_up[0], b_ffn_conv[0], b_ffn_down[0], s_len,
                 g_final=final_norm)
    return h.reshape(batch, s_len, d)
```

```python
import functools

import jax
import jax.numpy as jnp
from jax import lax
from jax.experimental import pallas as pl
from jax.experimental.pallas import tpu as pltpu

D_MODEL = 1024
HEAD_DIM = 64
N_SB_HEADS = 12
N_MEM_HEADS = 4
DIL_GROUPS = ((128, 1), (512, 4), (2048, 16))
HEADS_PER_GROUP = 4
N_DIL_HEADS = HEADS_PER_GROUP * len(DIL_GROUPS)
SB_WIDTH = N_SB_HEADS * HEAD_DIM
MEM_WIDTH = N_MEM_HEADS * HEAD_DIM
DIL_WIDTH = N_DIL_HEADS * HEAD_DIM
GROUP_WIDTH = HEADS_PER_GROUP * HEAD_DIM
D_FF = 2816
CONV_WIDTH = 3
EPS = 1e-6
ALIBI_MAX_BIAS = 8.0
QK_SCALE = HEAD_DIM ** -0.5
LOG2E = 1.4426950408889634

LANES = 128
BF16_ROWS = 16
VMEM_LIMIT_BYTES = 56 * 1024 * 1024

BLK = 128
F32 = jnp.float32
BF16 = jnp.bfloat16
NT_DIMS = (((1,), (1,)), ((), ()))


def _params(semantics):
    return pltpu.CompilerParams(dimension_semantics=semantics, vmem_limit_bytes=VMEM_LIMIT_BYTES)


def _resident(shape):
    return pl.BlockSpec(shape, lambda *_: (0,) * len(shape), pipeline_mode=pl.Buffered(1))


def _head0_lanes():
    return lax.broadcasted_iota(jnp.int32, (1, LANES), 1) < HEAD_DIM


def _split_heads(t, head0):
    zero = jnp.zeros_like(t)
    return jnp.concatenate([jnp.where(head0, t, zero), jnp.where(head0, zero, t)], axis=0)


def _with_den_cols(vst):
    row = lax.broadcasted_iota(jnp.int32, vst.shape, 0)
    lane = lax.broadcasted_iota(jnp.int32, vst.shape, 1)
    owns = (row < vst.shape[0] // 2) == (lane < HEAD_DIM)
    return jnp.concatenate([vst, jnp.where(owns, 1.0, 0.0).astype(vst.dtype)], axis=1)


def _rms_scale(x):
    return x * lax.rsqrt(jnp.mean(x * x, axis=-1, keepdims=True) + EPS)


PROJ_ROWS = 1024


def _rms_proj_kernel(n_out, x_ref, *refs):
    g_refs, w_refs, o_refs = refs[:n_out], refs[n_out:2 * n_out], refs[2 * n_out:]
    xhat = _rms_scale(x_ref[...])
    for g_ref, w_ref, o_ref in zip(g_refs, w_refs, o_refs):
        xn = (xhat * g_ref[...]).astype(BF16)
        o_ref[...] = jnp.dot(xn, w_ref[...], preferred_element_type=F32).astype(o_ref.dtype)


def _rms_proj(x, gains, weights, tm):
    t, d = x.shape
    n_out = len(gains)
    in_specs = [pl.BlockSpec((tm, d), lambda i: (i, 0))]
    in_specs += [_resident((1, d)) for _ in gains]
    in_specs += [_resident(w.shape) for w in weights]
    out_specs = [pl.BlockSpec((tm, w.shape[1]), lambda i: (i, 0)) for w in weights]
    out_shape = [jax.ShapeDtypeStruct((t, w.shape[1]), BF16) for w in weights]
    return pl.pallas_call(
        functools.partial(_rms_proj_kernel, n_out),
        grid=(t // tm,),
        in_specs=in_specs,
        out_specs=out_specs,
        out_shape=out_shape,
        compiler_params=_params(("parallel",)),
        name="rms_proj",
    )(x, *[g.reshape(1, d) for g in gains], *weights)


SB_QBLKS = 16
SB_ROWS = SB_QBLKS * BLK
SB_PAIRS = 6
SB_BAND = 3
SB_UNDERFLOW = 106.0
assert SB_BAND < SB_QBLKS


def _sb_kernel(q_ref, k_ref, v_ref, tri_ref, o_ref, carry_ref, acc_ref, lowest_ref):
    qt = pl.program_id(2)
    head0 = _head0_lanes()
    tri = tri_ref[...]
    pairs = range(SB_PAIRS)
    q_all = [q_ref[:, p * LANES:(p + 1) * LANES] * QK_SCALE for p in pairs]

    def key_block(p, k0, row_lo, row_hi, diagonal, carry, acc, live_from=None):
        cols = slice(p * LANES, (p + 1) * LANES)
        kst = _split_heads(k_ref[pl.ds(k0, BLK), cols], head0)
        vst = _split_heads(v_ref[pl.ds(k0, BLK), cols], head0)
        z = lax.dot_general(q_all[p][row_lo:row_hi], kst, NT_DIMS, preferred_element_type=F32)
        sp = jnp.maximum(z, 0.0) + jnp.log(1.0 + jnp.exp2(jnp.abs(z) * (-LOG2E)))
        if diagonal:
            t_rel = lax.broadcasted_iota(jnp.int32, (BLK, 2 * BLK), 0)
            s_rel = lax.broadcasted_iota(jnp.int32, (BLK, 2 * BLK), 1) & (BLK - 1)
            causal = s_rel < t_rel

            def mask(t):
                top = jnp.where(causal, t[:BLK], 0.0)
                return top if t.shape[0] == BLK else jnp.concatenate([top, t[BLK:]], axis=0)
        elif live_from is not None:
            live = lax.broadcasted_iota(jnp.int32, z.shape, 0) + row_lo >= live_from

            def mask(t):
                return jnp.where(live, t, 0.0)
        else:
            def mask(t):
                return t

        sp = mask(sp)
        sp16 = sp.astype(BF16)
        sums = [jnp.dot(sp16[:, h * BLK:(h + 1) * BLK], tri, preferred_element_type=F32)
                for h in range(2)]
        suffix = jnp.concatenate([s[:, :BLK] for s in sums], axis=1)
        total = jnp.concatenate([s[:, BLK:] for s in sums], axis=1)
        w = mask(jnp.exp2((z - (suffix + carry[row_lo:row_hi])) * LOG2E))
        pv = jnp.dot(w.astype(BF16), vst, preferred_element_type=F32)

        def all_rows(t):
            parts = [jnp.zeros((n, t.shape[1]), F32) if n else None for n in (row_lo, SB_ROWS - row_hi)]
            parts = [x for x in (parts[0], t, parts[1]) if x is not None]
            return t if len(parts) == 1 else jnp.concatenate(parts, axis=0)

        return carry + all_rows(total), acc + all_rows(pv)

    r0 = qt * SB_ROWS

    def band(lowest, pending_from):
        carry = [jnp.zeros((SB_ROWS, 2 * BLK), F32) for _ in pairs]
        acc = [jnp.zeros((SB_ROWS, LANES), F32) for _ in pairs]
        for j in range(SB_QBLKS - 1, lowest - 1, -1):
            for p in pairs:
                carry[p], acc[p] = key_block(p, pl.multiple_of(r0 + j * BLK, BLK), max(j, 0) * BLK,
                                             min(j + SB_BAND, SB_QBLKS) * BLK, j >= 0, carry[p], acc[p])
        smallest = None
        for p in pairs:
            carry_ref[p] = carry[p]
            acc_ref[p] = acc[p]
            o_ref[:, p * LANES:(p + 1) * LANES] = acc[p].astype(o_ref.dtype)
            m = jnp.min(carry[p][pending_from:])
            smallest = m if smallest is None else jnp.minimum(smallest, m)
        lowest_ref[0] = smallest

    @pl.when(qt == 0)
    def _():
        band(0, SB_BAND * BLK)

    @pl.when(qt > 0)
    def _():
        band(1 - SB_BAND, 0)

    top = SB_QBLKS - 1 - SB_BAND
    n_steps = top + 1 + qt * SB_QBLKS

    def step(state):
        j = top - state[0]
        lowest = None
        for p in pairs:
            carry, acc = key_block(p, pl.multiple_of(r0 + j * BLK, BLK), 0, SB_ROWS, False, carry_ref[p], acc_ref[p],
                                   live_from=(j + SB_BAND) * BLK)
            carry_ref[p] = carry
            acc_ref[p] = acc
            lowest = jnp.min(carry) if lowest is None else jnp.minimum(lowest, jnp.min(carry))
        return state[0] + 1, lowest

    @pl.when(lowest_ref[0] < SB_UNDERFLOW)
    def _():
        lax.while_loop(lambda s: jnp.logical_and(s[0] < n_steps, s[1] < SB_UNDERFLOW), step,
                       (jnp.int32(0), lowest_ref[0]))
        for p in pairs:
            o_ref[:, p * LANES:(p + 1) * LANES] = acc_ref[p].astype(o_ref.dtype)


def _sb_tri_weights():
    j = jnp.arange(BLK)[:, None]
    c = jnp.arange(2 * BLK)[None, :]
    return ((c >= BLK) | (j >= c)).astype(BF16)


def _sb_attn(proj, batch, s_len):
    width = SB_PAIRS * LANES
    n_groups = SB_WIDTH // width
    n_qt = s_len // SB_ROWS
    kv_blk = (s_len, width)
    return pl.pallas_call(
        _sb_kernel,
        grid=(batch, n_groups, n_qt),
        in_specs=[
            pl.BlockSpec((SB_ROWS, width), lambda b, p, t: (b * n_qt + t, p)),
            pl.BlockSpec(kv_blk, lambda b, p, t: (b, n_groups + p)),
            pl.BlockSpec(kv_blk, lambda b, p, t: (b, 2 * n_groups + p)),
            _resident((BLK, 2 * BLK)),
        ],
        out_specs=pl.BlockSpec((SB_ROWS, width), lambda b, p, t: (b * n_qt + t, p)),
        out_shape=jax.ShapeDtypeStruct((batch * s_len, SB_WIDTH), BF16),
        scratch_shapes=[pltpu.VMEM((SB_PAIRS, SB_ROWS, 2 * BLK), F32), pltpu.VMEM((SB_PAIRS, SB_ROWS, LANES), F32),
                        pltpu.SMEM((1,), F32)],
        compiler_params=_params(("parallel", "parallel", "arbitrary")),
        name="sb_attn",
    )(proj, proj, proj, _sb_tri_weights())


MEM_ROWS = 2048


def _mem_heads(q_ref, kv_ref, o_ref):
    s_len = q_ref.shape[0]
    mem_len = kv_ref.shape[0]
    head0 = _head0_lanes()
    n_pairs = MEM_WIDTH // LANES
    ksts = [_split_heads(kv_ref[:, p * LANES:(p + 1) * LANES], head0) for p in range(n_pairs)]
    vsts = [_with_den_cols(_split_heads(kv_ref[:, MEM_WIDTH + p * LANES:MEM_WIDTH + (p + 1) * LANES], head0))
            for p in range(n_pairs)]

    def step(c, carry):
        rows = pl.ds(pl.multiple_of(c * MEM_ROWS, MEM_ROWS), MEM_ROWS)
        for p in range(n_pairs):
            q = q_ref[rows, p * LANES:(p + 1) * LANES] * QK_SCALE
            z = lax.dot_general(q, ksts[p], NT_DIMS, preferred_element_type=F32)
            es = []
            for h in range(2):
                zh = z[:, h * mem_len:(h + 1) * mem_len]
                es.append(jnp.exp(zh - jnp.max(zh, axis=-1, keepdims=True)))
            od = jnp.dot(jnp.concatenate(es, axis=1).astype(BF16), vsts[p], preferred_element_type=F32)
            o_ref[rows, p * LANES:(p + 1) * LANES] = (od[:, :LANES] / od[:, LANES:]).astype(o_ref.dtype)
        return carry

    lax.fori_loop(0, s_len // MEM_ROWS, step, 0)


def _mem_attn(proj, q_col_block, kv_mem, batch, s_len):
    mem_len = kv_mem.shape[0] // batch
    return pl.pallas_call(
        _mem_heads,
        grid=(batch,),
        in_specs=[
            pl.BlockSpec((s_len, MEM_WIDTH), lambda b: (b, q_col_block)),
            pl.BlockSpec((mem_len, 2 * MEM_WIDTH), lambda b: (b, 0)),
        ],
        out_specs=pl.BlockSpec((s_len, MEM_WIDTH), lambda b: (b, 0)),
        out_shape=jax.ShapeDtypeStruct((batch * s_len, MEM_WIDTH), BF16),
        compiler_params=_params(("parallel",)),
        name="mem_attn",
    )(proj, kv_mem)


def _proj_b_kernel(x_ref, gkv_ref, gq_ref, wq_ref, wkv_ref, o0, o1, o2, stage_ref):
    tm = x_ref.shape[0]
    s_len = o0.shape[0]
    c = pl.program_id(1)
    gw = GROUP_WIDTH
    xhat = _rms_scale(x_ref[...])
    q = jnp.dot((xhat * gq_ref[...]).astype(BF16), wq_ref[...], preferred_element_type=F32)
    kv = jnp.dot((xhat * gkv_ref[...]).astype(BF16), wkv_ref[...], preferred_element_type=F32)
    for g, ((_, d), o_ref) in enumerate(zip(DIL_GROUPS, (o0, o1, o2))):
        qkv = jnp.concatenate([q[:, g * gw:(g + 1) * gw], kv[:, g * gw:(g + 1) * gw],
                               kv[:, DIL_WIDTH + g * gw:DIL_WIDTH + (g + 1) * gw]], axis=1)
        if d == 1:
            rows = pl.ds(pl.multiple_of(c * tm, tm), tm)
            o_ref[rows, :3 * gw] = qkv.astype(BF16)
            o_ref[rows, 3 * gw:] = q[:, DIL_WIDTH:].astype(BF16)
        else:
            n_cols = 3 * gw // LANES
            for j in range(n_cols):
                stage_ref[j] = qkv[:, j * LANES:(j + 1) * LANES]
            n = tm // d
            for r in range(d):
                dst = pl.ds(pl.multiple_of(r * (s_len // d) + c * n, n), n)
                o_ref[dst, :] = jnp.concatenate(
                    [stage_ref[j, pl.ds(r, n, stride=d), :] for j in range(n_cols)], axis=1).astype(BF16)


def _proj_b(h, g_kv, g_q, w_kv, w_in, batch, s_len):
    d = h.shape[1]
    tm = PROJ_ROWS
    n_tiles = s_len // tm
    widths = [3 * GROUP_WIDTH + MEM_WIDTH, 3 * GROUP_WIDTH, 3 * GROUP_WIDTH]
    return pl.pallas_call(
        _proj_b_kernel,
        grid=(batch, n_tiles),
        in_specs=[pl.BlockSpec((tm, d), lambda b, c: (b * n_tiles + c, 0)), _resident((1, d)), _resident((1, d)),
                  _resident(w_in.shape), _resident(w_kv.shape)],
        out_specs=[pl.BlockSpec((s_len, n), lambda b, c: (b, 0)) for n in widths],
        out_shape=[jax.ShapeDtypeStruct((batch * s_len, n), BF16) for n in widths],
        scratch_shapes=[pltpu.VMEM((3 * GROUP_WIDTH // LANES, tm, LANES), F32)],
        compiler_params=_params(("parallel", "arbitrary")),
        name="proj_b",
    )(h, g_kv.reshape(1, d), g_q.reshape(1, d), w_in.astype(BF16), w_kv.astype(BF16))


COMBINE_ROWS = 256


DIL_BLOCKS_PER_ITER = 16


def _softmax_parts(parts):
    m = parts[0]
    for t in parts[1:]:
        m = jnp.maximum(m, t)
    m = jnp.max(m, axis=-1, keepdims=True)
    return [jnp.exp(t - m) for t in parts], m


def _cross_attn_kernel(qkv0, qkv1, qkv2, kvm_ref, bias_first_ref, bias_ref, y0, y1, y2, om_ref,
                       o_cls, lse_cls, stage):
    s_len = qkv0.shape[0]
    head0 = _head0_lanes()
    gw = GROUP_WIDTH

    def block(g, qkv, p, row0, first):
        qc = slice(p * LANES, (p + 1) * LANES)
        kc = slice(gw + p * LANES, gw + (p + 1) * LANES)
        vc = slice(2 * gw + p * LANES, 2 * gw + (p + 1) * LANES)
        cur = pl.ds(pl.multiple_of(row0, BLK), BLK)
        q = qkv[cur, qc] * QK_SCALE
        kst = _split_heads(qkv[cur, kc], head0)
        vst = _with_den_cols(_split_heads(qkv[cur, vc], head0))
        if first:
            z = lax.dot_general(q, kst, NT_DIMS, preferred_element_type=F32) + bias_first_ref[g, p]
            heads = [[z[:, h * BLK:(h + 1) * BLK]] for h in range(2)]
        else:
            prev = pl.ds(pl.multiple_of(row0 - BLK, BLK), BLK)
            kst = jnp.concatenate([_split_heads(qkv[prev, kc], head0), kst], axis=0)
            vst = jnp.concatenate([_with_den_cols(_split_heads(qkv[prev, vc], head0)), vst], axis=0)
            z = lax.dot_general(q, kst, NT_DIMS, preferred_element_type=F32) + bias_ref[g, p]
            heads = [[z[:, h * BLK:(h + 1) * BLK], z[:, (2 + h) * BLK:(3 + h) * BLK]] for h in range(2)]
        (e0, m0), (e1, m1) = [_softmax_parts(parts) for parts in heads]
        pmat = jnp.concatenate([x for pair in zip(e0, e1) for x in pair], axis=1).astype(BF16)
        od = jnp.dot(pmat, vst, preferred_element_type=F32)
        o_cls[g, cur, qc] = od[:, :LANES] / od[:, LANES:]
        lse_cls[g, cur, qc] = jnp.where(head0, m0, m1) + jnp.log(od[:, LANES:])

    for g, ((_, d), qkv) in enumerate(zip(DIL_GROUPS, (qkv0, qkv1, qkv2))):
        blocks_per_class = s_len // d // BLK
        n_iters = s_len // BLK // DIL_BLOCKS_PER_ITER

        def blocks(it, starts_class, g=g, qkv=qkv, blocks_per_class=blocks_per_class):
            for k in range(DIL_BLOCKS_PER_ITER):
                if blocks_per_class <= DIL_BLOCKS_PER_ITER:
                    first = k % blocks_per_class == 0
                else:
                    first = k == 0 and starts_class
                for p in range(gw // LANES):
                    block(g, qkv, p, (it * DIL_BLOCKS_PER_ITER + k) * BLK, first)

        blocks(0, True)
        if n_iters > 1:
            lax.fori_loop(1, n_iters, lambda it, x, blocks=blocks: (blocks(it, False), x)[1], 0)

    def natural_rows(src, g, c, k):
        d = DIL_GROUPS[g][1]
        if d == 1:
            return src[g, pl.ds(pl.multiple_of(c * COMBINE_ROWS, COMBINE_ROWS), COMBINE_ROWS), :]
        n = COMBINE_ROWS // d
        n_cols = gw // LANES
        for r in range(d):
            rows = pl.ds(pl.multiple_of(r * (s_len // d) + c * n, n), n)
            for j in range(n_cols):
                stage[k * n_cols + j, pl.ds(r, n, stride=d), :] = src[g, rows, j * LANES:(j + 1) * LANES]
        return jnp.concatenate([stage[k * n_cols + j] for j in range(n_cols)], axis=1)

    def combine(c, carry):
        ls = [natural_rows(lse_cls, g, c, g) for g in range(3)]
        m = jnp.maximum(jnp.maximum(ls[0], ls[1]), ls[2])
        es = [jnp.exp(l - m) for l in ls]
        tot = es[0] + es[1] + es[2]
        rows = pl.ds(pl.multiple_of(c * COMBINE_ROWS, COMBINE_ROWS), COMBINE_ROWS)
        for g, y_ref in enumerate((y0, y1, y2)):
            y_ref[rows, :] = (natural_rows(o_cls, g, c, 3 + g) * (es[g] / tot)).astype(y_ref.dtype)
        return carry

    lax.fori_loop(0, s_len // COMBINE_ROWS, combine, 0)
    _mem_heads(qkv0.at[:, 3 * gw:], kvm_ref, om_ref)


def _dil_biases(slopes):
    i = jnp.arange(BLK)[:, None]
    j = jnp.arange(2 * BLK)[None, :]
    delta = i + BLK - j
    valid = (delta >= 0) & (delta <= BLK)
    firsts, others = [], []
    for g, (_, d) in enumerate(DIL_GROUPS):
        dist = (delta * d).astype(F32)
        f_p, o_p = [], []
        for p in range(2):
            halves = [jnp.where(valid, -slopes[g * HEADS_PER_GROUP + 2 * p + h] * dist, -jnp.inf) for h in range(2)]
            f_p.append(jnp.concatenate([halves[0][:, BLK:], halves[1][:, BLK:]], axis=1))
            o_p.append(jnp.concatenate([halves[0][:, :BLK], halves[1][:, :BLK],
                                        halves[0][:, BLK:], halves[1][:, BLK:]], axis=1))
        firsts.append(jnp.stack(f_p))
        others.append(jnp.stack(o_p))
    return jnp.stack(firsts), jnp.stack(others)


def _cross_attn(qkvs, kv_mem, batch, s_len):
    mem_len = kv_mem.shape[0] // batch
    slopes = 2.0 ** (-ALIBI_MAX_BIAS * jnp.arange(1, N_DIL_HEADS + 1, dtype=F32) / N_DIL_HEADS)
    bias_first, bias = _dil_biases(slopes)
    seq = lambda n: pl.BlockSpec((s_len, n), lambda b: (b, 0))
    n_out = len(DIL_GROUPS) + 1
    return pl.pallas_call(
        _cross_attn_kernel,
        grid=(batch,),
        in_specs=[seq(a.shape[1]) for a in qkvs]
        + [pl.BlockSpec((mem_len, 2 * MEM_WIDTH), lambda b: (b, 0)), _resident(bias_first.shape), _resident(bias.shape)],
        out_specs=[seq(GROUP_WIDTH)] * n_out,
        out_shape=[jax.ShapeDtypeStruct((batch * s_len, GROUP_WIDTH), BF16)] * n_out,
        scratch_shapes=[pltpu.VMEM((len(DIL_GROUPS), s_len, GROUP_WIDTH), F32)] * 2
        + [pltpu.VMEM((2 * len(DIL_GROUPS) * GROUP_WIDTH // LANES, COMBINE_ROWS, LANES), F32)],
        compiler_params=_params(("parallel",)),
        name="cross_attn",
    )(*qkvs, kv_mem, bias_first, bias)


SUBLANES = 8
FFN_ROWS = 512
FFN_CHUNK = 256
FFN_VREG_ROWS = FFN_ROWS // SUBLANES
FFN_PITCH = FFN_VREG_ROWS + SUBLANES
FFN_DOWN_PARTS = 4


def _out_ffn_kernel(n_attn, tiles_per_seq, final, h_ref, *refs):
    a_refs, wo_refs = refs[:n_attn], refs[n_attn:2 * n_attn]
    g_ref, wup_ref, wconv_ref, wdown_ref = refs[2 * n_attn:2 * n_attn + 4]
    rest = refs[2 * n_attn + 4:]
    if final:
        gf_ref, out_ref, stage_ref, xe_ref, act_ref, tail_ref = rest
    else:
        out_ref, stage_ref, xe_ref, act_ref, tail_ref = rest
    n_cols = h_ref.shape[1] // LANES
    nv = FFN_VREG_ROWS
    i = pl.program_id(0)

    @pl.when(i == 0)
    def _():
        tail_ref[...] = jnp.zeros_like(tail_ref)

    h = h_ref[...]
    for a_ref, wo_ref in zip(a_refs, wo_refs):
        h = h + jnp.dot(a_ref[...], wo_ref[...], preferred_element_type=F32)
    xn = _rms_scale(h) * g_ref[...]

    for c in range(n_cols):
        for s in range(SUBLANES):
            stage_ref[c, s * FFN_PITCH:s * FFN_PITCH + nv, :] = xn[s * nv:(s + 1) * nv, c * LANES:(c + 1) * LANES]
    for jj in range(nv * SUBLANES // BF16_ROWS):
        vregs = [jnp.concatenate([stage_ref[c, pl.ds(j, SUBLANES, stride=FFN_PITCH), :] for c in range(n_cols)], axis=1)
                 for j in range(jj * BF16_ROWS // SUBLANES, (jj + 1) * BF16_ROWS // SUBLANES)]
        xe_ref[jj * BF16_ROWS:(jj + 1) * BF16_ROWS, :] = jnp.concatenate(vregs, axis=0).astype(xe_ref.dtype)

    starts_seq = i % tiles_per_seq == 0
    first_sublane = lax.broadcasted_iota(jnp.int32, (SUBLANES, FFN_CHUNK), 0) == 0

    def conv(u, wc, tail):
        def before_first(prev_vreg, last_vreg):
            return jnp.where(first_sublane, pltpu.roll(prev_vreg, 1, axis=0), pltpu.roll(last_vreg, 1, axis=0))

        back1 = before_first(tail[SUBLANES:], u[-SUBLANES:])
        back2 = before_first(tail[:SUBLANES], u[-2 * SUBLANES:-SUBLANES])
        u1 = jnp.concatenate([back1, u[:-SUBLANES]], axis=0)
        u2 = jnp.concatenate([back2, back1, u[:-2 * SUBLANES]], axis=0)
        return wc[0:1] * u2 + wc[1:2] * u1 + wc[2:3] * u

    xe = xe_ref[...]
    for c in range(D_FF // FFN_CHUNK):
        convs = []
        for part in range(2):
            cols = slice(part * D_FF + c * FFN_CHUNK, part * D_FF + (c + 1) * FFN_CHUNK)
            u = jnp.dot(xe, wup_ref[:, cols], preferred_element_type=F32)
            tail = tail_ref[2 * c + part]
            tail_ref[2 * c + part] = u[-2 * SUBLANES:]
            convs.append(conv(u, wconv_ref[:, cols], jnp.where(starts_seq, jnp.zeros_like(tail), tail)))
        half = 0.5 * convs[1]
        act_ref[:, c * FFN_CHUNK:(c + 1) * FFN_CHUNK] = ((half + half * jnp.tanh(half)) * convs[0]).astype(act_ref.dtype)

    act = act_ref[...]
    cols_per_part = n_cols // FFN_DOWN_PARTS
    y_cols = []
    for part in range(FFN_DOWN_PARTS):
        c_lo, c_hi = part * cols_per_part, (part + 1) * cols_per_part
        lhs = act
        if part >= 2:
            done = y_cols[(part - 2) * cols_per_part][:BF16_ROWS]
            zero = ((pltpu.bitcast(done, jnp.uint32) >> 16) >> 16).astype(act.dtype)
            corner = jnp.concatenate([act[:BF16_ROWS, :LANES] + zero, act[:BF16_ROWS, LANES:]], axis=1)
            lhs = jnp.concatenate([corner, act[BF16_ROWS:]], axis=0)
        yp = jnp.dot(lhs, wdown_ref[:, c_lo * LANES:c_hi * LANES], preferred_element_type=F32)
        for c in range(c_lo, c_hi):
            for j in range(nv):
                stage_ref[c, pl.ds(j, SUBLANES, stride=FFN_PITCH), :] = yp[j * SUBLANES:(j + 1) * SUBLANES,
                                                                           (c - c_lo) * LANES:(c - c_lo + 1) * LANES]
            y_cols.append(jnp.concatenate([stage_ref[c, s * FFN_PITCH:s * FFN_PITCH + nv, :] for s in range(SUBLANES)],
                                          axis=0))
    y = h + jnp.concatenate(y_cols, axis=1)
    if final:
        y = _rms_scale(y) * gf_ref[...]
    out_ref[...] = y


def _out_ffn(h, attn_parts, w_out, g_ffn, w_up, w_conv, w_down, s_len, g_final=None):
    t, d = h.shape
    tm = FFN_ROWS
    n_attn = len(attn_parts)
    final = g_final is not None
    offs = [0]
    for a in attn_parts:
        offs.append(offs[-1] + a.shape[1])
    wo_parts = [w_out[offs[k]:offs[k + 1]].astype(BF16) for k in range(n_attn)]
    row = lambda n: pl.BlockSpec((tm, n), lambda i: (i, 0))
    in_specs = [row(d)] + [row(a.shape[1]) for a in attn_parts] + [_resident(w.shape) for w in wo_parts]
    in_specs += [_resident((1, d)), _resident(w_up.shape), _resident(w_conv.shape), _resident(w_down.shape)]
    args = [h, *attn_parts, *wo_parts, g_ffn.reshape(1, d), w_up.astype(BF16), w_conv, w_down.astype(BF16)]
    if final:
        in_specs.append(_resident((1, d)))
        args.append(g_final.reshape(1, d))
    return pl.pallas_call(
        functools.partial(_out_ffn_kernel, n_attn, s_len // tm, final),
        grid=(t // tm,),
        in_specs=in_specs,
        out_specs=row(d),
        out_shape=jax.ShapeDtypeStruct((t, d), F32),
        scratch_shapes=[pltpu.VMEM((d // LANES, SUBLANES * FFN_PITCH, LANES), F32), pltpu.VMEM((tm, d), BF16),
                        pltpu.VMEM((tm, D_FF), BF16),
                        pltpu.VMEM((2 * D_FF // FFN_CHUNK, 2 * SUBLANES, FFN_CHUNK), F32)],
        compiler_params=_params(("arbitrary",)),
        name="out_ffn",
    )(*args)


def kernel(x, mem, a_norm_attn, a_w_in, a_w_out, a_norm_mem, a_w_mem_kv, a_norm_ffn, a_ffn_up, a_ffn_conv, a_ffn_down, kv_norm, w_kv_shared, b_norm_attn, b_w_in, b_w_out, b_norm_mem, b_w_mem_kv, b_norm_ffn, b_ffn_up, b_ffn_conv, b_ffn_down, final_norm):
    batch, s_len, d = x.shape
    assert a_w_in.shape[0] == 1 and b_w_in.shape[0] == 1, "one self-decoder and one cross-decoder layer"
    assert d == D_MODEL and s_len % (BLK * DIL_GROUPS[-1][1]) == 0 and s_len % FFN_ROWS == 0
    assert s_len % SB_ROWS == 0 and s_len % MEM_ROWS == 0 and s_len % PROJ_ROWS == 0
    assert (batch * mem.shape[1]) % PROJ_ROWS == 0
    t = batch * s_len
    h = x.reshape(t, d)
    mem2 = mem.reshape(batch * mem.shape[1], d)

    kvm_a, kvm_b = _rms_proj(mem2, [a_norm_mem[0], b_norm_mem[0]],
                             [a_w_mem_kv[0].astype(BF16), b_w_mem_kv[0].astype(BF16)], PROJ_ROWS)

    (proj_a,) = _rms_proj(h, [a_norm_attn[0]], [a_w_in[0].astype(BF16)], PROJ_ROWS)
    o_sb = _sb_attn(proj_a, batch, s_len)
    o_mem = _mem_attn(proj_a, 3 * SB_WIDTH // MEM_WIDTH, kvm_a, batch, s_len)
    h = _out_ffn(h, [o_sb, o_mem], a_w_out[0], a_norm_ffn[0], a_ffn_up[0], a_ffn_conv[0], a_ffn_down[0], s_len)

    qkvs = _proj_b(h, kv_norm, b_norm_attn[0], w_kv_shared, b_w_in[0], batch, s_len)
    attn_parts = _cross_attn(qkvs, kvm_b, batch, s_len)
    h = _out_ffn(h, attn_parts, b_w_out[0], b_norm_ffn[0], b_ffn_up[0], b_ffn_conv[0], b_ffn_down[0], s_len,
                 g_final=final_norm)
    return h.reshape(batch, s_len, d)
```
